```python
import jax, jax.numpy as jnp
from jax import lax
import numpy as np

D_MODEL = 2048
BATCH = 4
SEQ = 4096
DEPTH = 1

SSM_EXPAND = 2
D_INNER = SSM_EXPAND * D_MODEL
SSM_HEAD_DIM = 64
SSM_HEADS = D_INNER // SSM_HEAD_DIM
SSM_GROUPS = 8
SSM_STATE = 128
SSM_CONV = 4
SSM_CHUNK = 128
SSM_GN = SSM_GROUPS * SSM_STATE
SSM_CONV_DIM = D_INNER + 2 * SSM_GN
ATTN_HEADS = 32
ATTN_KV_HEADS = 4
ATTN_HEAD_DIM = 64
ATTN_GROUP = ATTN_HEADS // ATTN_KV_HEADS
ATTN_WIDTH = ATTN_HEADS * ATTN_HEAD_DIM
KV_WIDTH = ATTN_KV_HEADS * ATTN_HEAD_DIM
WINDOW = 128
N_BRANCH = 2
D_FF = 5632
FFN_CONV = 3
NORM_EPS = 1e-5
IN_SPLITS = (D_INNER, D_INNER + SSM_CONV_DIM, D_INNER + SSM_CONV_DIM + SSM_HEADS,
             D_INNER + SSM_CONV_DIM + SSM_HEADS + ATTN_WIDTH,
             D_INNER + SSM_CONV_DIM + SSM_HEADS + ATTN_WIDTH + KV_WIDTH,
             D_INNER + SSM_CONV_DIM + SSM_HEADS + ATTN_WIDTH + 2 * KV_WIDTH)
IN_PROJ_DIM = D_INNER + SSM_CONV_DIM + SSM_HEADS + ATTN_WIDTH + 2 * KV_WIDTH + N_BRANCH * D_MODEL

kernel_name = "hybrid_ssd_swa_gated_convffn"


def rms_norm(x, w):
    xf = x.astype(jnp.float32)
    y = xf * lax.rsqrt(jnp.mean(xf * xf, axis=-1, keepdims=True) + NORM_EPS)
    return (y * w.astype(jnp.float32)).astype(x.dtype)


def causal_dwconv(x, w, b):
    k, c = w.shape
    y = lax.conv_general_dilated(x, w[:, None, :].astype(x.dtype), window_strides=(1,),
                                 padding=[(k - 1, 0)], dimension_numbers=('NWC', 'WIO', 'NWC'),
                                 feature_group_count=c)
    return y + b.astype(x.dtype)


def ssd_chunked(xh, dt, a, bm, cm):
    f32 = jnp.float32
    b, s, h, p = xh.shape
    g, n = bm.shape[-2:]
    r = h // g
    l = SSM_CHUNK
    c = s // l
    x = (xh.astype(f32) * dt[..., None]).reshape(b, c, l, g, r, p)
    ad = (dt * a).reshape(b, c, l, g, r).transpose(0, 3, 4, 1, 2)
    bc = bm.astype(f32).reshape(b, c, l, g, n)
    cc = cm.astype(f32).reshape(b, c, l, g, n)
    a_cum = jnp.cumsum(ad, axis=-1)
    causal = jnp.tril(jnp.ones((l, l), dtype=bool))
    seg = a_cum[..., :, None] - a_cum[..., None, :]
    decay = jnp.exp(jnp.where(causal, seg, -jnp.inf))
    cb = jnp.einsum('bclgn,bcsgn->bcgls', cc, bc)
    y_diag = jnp.einsum('bcgls,bgrcls,bcsgrp->bclgrp', cb, decay, x)
    decay_states = jnp.exp(a_cum[..., -1:] - a_cum)
    chunk_states = jnp.einsum('bclgn,bgrcl,bclgrp->cbgrpn', bc, decay_states, x)
    chunk_decay = jnp.exp(a_cum[..., -1]).transpose(3, 0, 1, 2)

    def step(state, inp):
        st_c, dec_c = inp
        return state * dec_c[..., None, None] + st_c, state

    init = jnp.zeros((b, g, r, p, n), f32)
    _, prev_states = lax.scan(step, init, (chunk_states, chunk_decay))
    y_off = jnp.einsum('bclgn,cbgrpn,bgrcl->bclgrp', cc, prev_states, jnp.exp(a_cum))
    return (y_diag + y_off).reshape(b, s, h, p)


def sliding_window_attention(q, k, v, sinks):
    f32 = jnp.float32
    b, s, _ = q.shape
    nb = s // WINDOW
    qb = q.reshape(b, nb, WINDOW, ATTN_KV_HEADS, ATTN_GROUP, ATTN_HEAD_DIM)
    kb = k.reshape(b, nb, WINDOW, ATTN_KV_HEADS, ATTN_HEAD_DIM)
    vb = v.reshape(b, nb, WINDOW, ATTN_KV_HEADS, ATTN_HEAD_DIM)
    shift = ((0, 0), (1, 0), (0, 0), (0, 0), (0, 0))
    kcat = jnp.concatenate([jnp.pad(kb[:, :-1], shift), kb], axis=2)
    vcat = jnp.concatenate([jnp.pad(vb[:, :-1], shift), vb], axis=2)
    scores = jnp.einsum('bnqkrd,bnjkd->bkrnqj', qb, kcat).astype(f32) * (ATTN_HEAD_DIM ** -0.5)
    qi = jnp.arange(WINDOW)[:, None]
    kj = jnp.arange(2 * WINDOW)[None, :]
    dist = (qi - kj + WINDOW)
    key_pos = jnp.arange(nb)[:, None, None] * WINDOW - WINDOW + kj[None]
    valid = (dist >= 0) & (dist < WINDOW) & (key_pos >= 0)
    slopes = jnp.exp2(-8.0 / ATTN_HEADS * jnp.arange(1, ATTN_HEADS + 1, dtype=f32))
    bias = -slopes.reshape(ATTN_KV_HEADS, ATTN_GROUP)[:, :, None, None, None] * dist.astype(f32)
    logits = jnp.where(valid, scores + bias, -jnp.inf)
    sink = sinks.astype(f32).reshape(1, ATTN_KV_HEADS, ATTN_GROUP, 1, 1, 1)
    m = jnp.maximum(jnp.max(logits, axis=-1, keepdims=True), sink)
    pexp = jnp.exp(logits - m)
    probs = pexp / (jnp.sum(pexp, axis=-1, keepdims=True) + jnp.exp(sink - m))
    out = jnp.einsum('bkrnqj,bnjkd->bnqkrd', probs.astype(v.dtype), vcat)
    return out.reshape(b, s, ATTN_WIDTH)


def token_mixer(u, w_in, ssm_conv_w, ssm_conv_b, dt_bias, a_log, d_skip, ssm_norm_w,
                attn_sinks, w_ssm_out, w_attn_out, w_o):
    f32 = jnp.float32
    b, s, _ = u.shape
    z, xbc, dt_raw, q, k, v, gate_logits = jnp.split(u @ w_in, IN_SPLITS, axis=-1)
    xbc = jax.nn.silu(causal_dwconv(xbc, ssm_conv_w, ssm_conv_b))
    xs, bm, cm = jnp.split(xbc, (D_INNER, D_INNER + SSM_GN), axis=-1)
    dt = jax.nn.softplus(dt_raw.astype(f32) + dt_bias.astype(f32))
    a = -jnp.exp(a_log.astype(f32))
    xh = xs.reshape(b, s, SSM_HEADS, SSM_HEAD_DIM)
    y = ssd_chunked(xh, dt, a, bm.reshape(b, s, SSM_GROUPS, SSM_STATE),
                    cm.reshape(b, s, SSM_GROUPS, SSM_STATE))
    y = y + d_skip.astype(f32)[:, None] * xh.astype(f32)
    gs = D_INNER // SSM_GROUPS
    y = y.reshape(b, s, SSM_GROUPS, gs) * jax.nn.silu(z.astype(f32)).reshape(b, s, SSM_GROUPS, gs)
    y = y * lax.rsqrt(jnp.mean(y * y, axis=-1, keepdims=True) + NORM_EPS)
    y = (y.reshape(b, s, D_INNER) * ssm_norm_w.astype(f32)).astype(u.dtype)
    y_ssm = y @ w_ssm_out
    y_attn = sliding_window_attention(q, k, v, attn_sinks) @ w_attn_out
    g_ssm, g_attn = jnp.split(jax.nn.sigmoid(gate_logits), N_BRANCH, axis=-1)
    return (g_ssm * y_ssm + g_attn * y_attn) @ w_o


def conv_ffn(u, w_ffn_in, ffn_conv_w, ffn_conv_b, w_ffn_out):
    gate, val = jnp.split(u @ w_ffn_in, 2, axis=-1)
    h = jax.nn.gelu(causal_dwconv(gate, ffn_conv_w, ffn_conv_b), approximate=False) * val
    return h @ w_ffn_out


def setup_inputs(seed: int = 0) -> dict:
    key = jax.random.key(seed)
    ks = jax.random.split(key, 20)
    f32 = jnp.float32
    L = DEPTH

    def nrm(k, shape, scale):
        return jax.random.normal(k, shape, f32) * scale

    dt0 = jnp.exp(jax.random.uniform(ks[5], (L, SSM_HEADS), f32, np.log(1e-3), np.log(1e-1)))
    dt_bias = dt0 + jnp.log(-jnp.expm1(-dt0))
    return {
        'x': nrm(ks[0], (BATCH, SEQ, D_MODEL), 1.0),
        'attn_norm_w': 1.0 + nrm(ks[1], (L, D_MODEL), 0.02),
        'w_in': nrm(ks[2], (L, D_MODEL, IN_PROJ_DIM), D_MODEL ** -0.5),
        'ssm_conv_w': nrm(ks[3], (L, SSM_CONV, SSM_CONV_DIM), SSM_CONV ** -0.5),
        'ssm_conv_b': nrm(ks[4], (L, SSM_CONV_DIM), 0.02),
        'dt_bias': dt_bias,
        'a_log': jnp.log(jax.random.uniform(ks[6], (L, SSM_HEADS), f32, 1.0, 16.0)),
        'd_skip': 1.0 + nrm(ks[7], (L, SSM_HEADS), 0.02),
        'ssm_norm_w': 1.0 + nrm(ks[8], (L, D_INNER), 0.02),
        'attn_sinks': nrm(ks[9], (L, ATTN_HEADS), 0.5),
        'w_ssm_out': nrm(ks[10], (L, D_INNER, D_MODEL), D_INNER ** -0.5),
        'w_attn_out': nrm(ks[11], (L, ATTN_WIDTH, D_MODEL), ATTN_WIDTH ** -0.5),
        'w_o': nrm(ks[12], (L, D_MODEL, D_MODEL), D_MODEL ** -0.5),
        'ffn_norm_w': 1.0 + nrm(ks[13], (L, D_MODEL), 0.02),
        'w_ffn_in': nrm(ks[14], (L, D_MODEL, 2 * D_FF), D_MODEL ** -0.5),
        'ffn_conv_w': nrm(ks[15], (L, FFN_CONV, D_FF), FFN_CONV ** -0.5),
        'ffn_conv_b': nrm(ks[16], (L, D_FF), 0.02),
        'w_ffn_out': nrm(ks[17], (L, D_FF, D_MODEL), D_FF ** -0.5),
        'final_norm_w': 1.0 + nrm(ks[18], (D_MODEL,), 0.02),
    }


def reference(x, attn_norm_w, w_in, ssm_conv_w, ssm_conv_b, dt_bias, a_log, d_skip, ssm_norm_w,
              attn_sinks, w_ssm_out, w_attn_out, w_o, ffn_norm_w, w_ffn_in, ffn_conv_w, ffn_conv_b,
              w_ffn_out, final_norm_w):
    h = x
    for i in range(DEPTH):
        u = rms_norm(h, attn_norm_w[i])
        h = h + token_mixer(u, w_in[i], ssm_conv_w[i], ssm_conv_b[i], dt_bias[i], a_log[i],
                            d_skip[i], ssm_norm_w[i], attn_sinks[i], w_ssm_out[i],
                            w_attn_out[i], w_o[i])
        u = rms_norm(h, ffn_norm_w[i])
        h = h + conv_ffn(u, w_ffn_in[i], ffn_conv_w[i], ffn_conv_b[i], w_ffn_out[i])
    return rms_norm(h, final_norm_w)
```

```python
import functools
import math

import numpy as np
import jax
import jax.numpy as jnp
from jax import lax
from jax.experimental import pallas as pl
from jax.experimental.pallas import tpu as pltpu

F32 = jnp.float32
BF16 = jnp.bfloat16

D_MODEL = 2048
D_INNER = 4096
SSM_HEADS = 64
SSM_HEAD_DIM = 64
SSM_GROUPS = 8
SSM_STATE = 128
SSM_CONV = 4
CHUNK = 128
GROUP_WIDTH = D_INNER // SSM_GROUPS
HEADS_PER_GROUP = SSM_HEADS // SSM_GROUPS
SSM_GN = SSM_GROUPS * SSM_STATE
ATTN_HEADS = 32
ATTN_KV_HEADS = 4
ATTN_GROUP = ATTN_HEADS // ATTN_KV_HEADS
ATTN_HEAD_DIM = 64
ATTN_WIDTH = ATTN_HEADS * ATTN_HEAD_DIM
KV_WIDTH = ATTN_KV_HEADS * ATTN_HEAD_DIM
WINDOW = 128
D_FF = 5632
FFN_CONV = 3
NORM_EPS = 1e-5

OFF_XBC = D_INNER
OFF_DT = D_INNER + D_INNER + 2 * SSM_GN
OFF_Q = OFF_DT + SSM_HEADS
OFF_GATE = OFF_Q + ATTN_WIDTH + 2 * KV_WIDTH
QKV_WIDTH = ATTN_WIDTH + 2 * KV_WIDTH
PF_WIDTH = OFF_DT + 2 * D_MODEL

LANE = 128
HALO = 8
FFN_HALO = 16
VMEM_LIMIT = 56 * 1024 * 1024


def _cparams(semantics):
    return pltpu.CompilerParams(dimension_semantics=semantics, vmem_limit_bytes=VMEM_LIMIT)


def _dot(a, b):
    return jnp.dot(a, b, preferred_element_type=F32)


def _dot_nt(a, b):
    return lax.dot_general(a, b, (((1,), (1,)), ((), ())), preferred_element_type=F32)


def _split3(v):
    hi = v.astype(BF16)
    r1 = v - hi.astype(F32)
    mid = r1.astype(BF16)
    lo = (r1 - mid.astype(F32)).astype(BF16)
    return hi, mid, lo


def _dot_exact_rhs(v, m):
    hi, mid, lo = _split3(v)
    return _dot(hi, m) + _dot(mid, m) + _dot(lo, m)


def _dot_exact_lhs(m, v):
    hi, mid, lo = _split3(v)
    return _dot(m, hi) + _dot(m, mid) + _dot(m, lo)


def _rms_scale(x, w):
    ms = jnp.mean(x * x, axis=-1, keepdims=True)
    return x * lax.rsqrt(ms + NORM_EPS) * w


def _norm_proj_kernel(x_ref, nw_ref, w_ref, o_ref, u_scr):
    @pl.when(pl.program_id(1) == 0)
    def _():
        u_scr[...] = _rms_scale(x_ref[...], nw_ref[...]).astype(BF16)

    o_ref[...] = _dot(u_scr[...], w_ref[...]).astype(o_ref.dtype)


def _norm_proj(x2, nw, w, out_dtype, tm, tn):
    t, d = x2.shape
    n = w.shape[1]
    return pl.pallas_call(
        _norm_proj_kernel,
        grid=(t // tm, n // tn),
        in_specs=[
            pl.BlockSpec((tm, d), lambda i, j: (i, 0)),
            pl.BlockSpec((1, d), lambda i, j: (0, 0)),
            pl.BlockSpec((d, tn), lambda i, j: (0, j)),
        ],
        out_specs=pl.BlockSpec((tm, tn), lambda i, j: (i, j)),
        out_shape=jax.ShapeDtypeStruct((t, n), out_dtype),
        scratch_shapes=[pltpu.VMEM((tm, d), BF16)],
        compiler_params=_cparams(("parallel", "arbitrary")),
        name="norm_proj",
    )(x2, nw, w)


def _conv_silu(cur, halo_ref, w_ref, b_ref):
    ext = jnp.concatenate([halo_ref[...], cur], axis=0)
    w = w_ref[0]
    acc = b_ref[0] + w[SSM_CONV - 1:SSM_CONV] * cur
    for k in range(SSM_CONV - 1):
        lo = HALO - (SSM_CONV - 1) + k
        acc = acc + w[k:k + 1] * ext[lo:lo + CHUNK]
    halo_ref[...] = cur[CHUNK - HALO:]
    return acc * jax.nn.sigmoid(acc)


def _ssd_kernel(z_ref, xs_ref, b_ref, c_ref, dt_ref, cwx_ref, cbx_ref, cwb_ref, cbb_ref,
                cwc_ref, cbc_ref, sel_ref, dtb_ref, a_ref, eloc_ref, dsk_ref, nw_ref,
                y_ref, state_scr, hx_scr, hb_scr, hc_scr):
    @pl.when(pl.program_id(2) == 0)
    def _():
        state_scr[...] = jnp.zeros_like(state_scr)
        hx_scr[...] = jnp.zeros_like(hx_scr)
        hb_scr[...] = jnp.zeros_like(hb_scr)
        hc_scr[...] = jnp.zeros_like(hc_scr)

    xs = _conv_silu(xs_ref[...], hx_scr, cwx_ref, cbx_ref)
    bm = _conv_silu(b_ref[...], hb_scr, cwb_ref, cbb_ref)
    cm = _conv_silu(c_ref[...], hc_scr, cwc_ref, cbc_ref)

    dt_raw = _dot_exact_rhs(dt_ref[...], sel_ref[0])
    v = dt_raw + dtb_ref[0]
    dt = jnp.maximum(v, 0.0) + jnp.log1p(jnp.exp(-jnp.abs(v)))
    ad = dt * a_ref[0]

    row = lax.broadcasted_iota(jnp.int32, (CHUNK, CHUNK), 0)
    col = lax.broadcasted_iota(jnp.int32, (CHUNK, CHUNK), 1)
    causal = row >= col
    tril = causal.astype(BF16)
    a_cum = _dot_exact_lhs(tril, ad)
    a_cum_t = a_cum.T

    ea = jnp.exp(a_cum)
    ds = jnp.exp(a_cum[CHUNK - 1:CHUNK, :] - a_cum)
    eloc = eloc_ref[...]
    dtx = _dot_exact_rhs(dt, eloc)
    eax = _dot_exact_rhs(ea, eloc)
    dsx = _dot_exact_rhs(ds, eloc)

    xdt = xs * dtx
    bm16 = bm.astype(BF16)
    cm16 = cm.astype(BF16)
    cb = _dot_nt(cm16, bm16)

    lane = lax.broadcasted_iota(jnp.int32, (CHUNK, LANE), 1)
    low_half = lane < SSM_HEAD_DIM
    neg_inf = jnp.float32(-jnp.inf)
    y_parts = []
    for pr in range(HEADS_PER_GROUP // 2):
        ms = []
        for r in (2 * pr, 2 * pr + 1):
            seg = a_cum[:, r:r + 1] - a_cum_t[r:r + 1, :]
            decay = jnp.exp(jnp.where(causal, seg, neg_inf))
            ms.append((cb * decay).astype(BF16))
        m_pair = jnp.concatenate(ms, axis=1)
        xp = xdt[:, pr * LANE:(pr + 1) * LANE]
        x_pair = jnp.concatenate([jnp.where(low_half, xp, 0.0), jnp.where(low_half, 0.0, xp)],
                                 axis=0).astype(BF16)
        y_parts.append(_dot(m_pair, x_pair))
    y = jnp.concatenate(y_parts, axis=1)

    state = state_scr[...]
    y = y + _dot(cm16, state.astype(BF16)) * eax
    new_state = _dot(bm.T.astype(BF16), (xdt * dsx).astype(BF16))
    state_scr[...] = state * eax[CHUNK - 1:CHUNK, :] + new_state

    y = y + dsk_ref[0] * xs
    zz = z_ref[...]
    y = y * (zz * jax.nn.sigmoid(zz))
    y_ref[...] = _rms_scale(y, nw_ref[0]).astype(y_ref.dtype)


def _ssd(pf, dt_raw, cwx, cbx, cwb, cbb, cwc, cbc, sel, dtb, a_neg, eloc, dsk, nw, batch, seq):
    t = pf.shape[0]
    nc = seq // CHUNK
    g = SSM_GROUPS
    xs_blk0 = OFF_XBC // GROUP_WIDTH
    b_blk0 = (OFF_XBC + D_INNER) // SSM_STATE
    c_blk0 = b_blk0 + SSM_GROUPS

    def rows(b, gi, c):
        return b * nc + c

    per_group = lambda shape: pl.BlockSpec((1,) + shape, lambda b, gi, c: (gi, 0, 0))
    return pl.pallas_call(
        _ssd_kernel,
        grid=(batch, g, nc),
        in_specs=[
            pl.BlockSpec((CHUNK, GROUP_WIDTH), lambda b, gi, c: (rows(b, gi, c), gi)),
            pl.BlockSpec((CHUNK, GROUP_WIDTH), lambda b, gi, c: (rows(b, gi, c), xs_blk0 + gi)),
            pl.BlockSpec((CHUNK, SSM_STATE), lambda b, gi, c: (rows(b, gi, c), b_blk0 + gi)),
            pl.BlockSpec((CHUNK, SSM_STATE), lambda b, gi, c: (rows(b, gi, c), c_blk0 + gi)),
            pl.BlockSpec((CHUNK, LANE), lambda b, gi, c: (rows(b, gi, c), 0)),
            per_group((SSM_CONV, GROUP_WIDTH)), per_group((1, GROUP_WIDTH)),
            per_group((SSM_CONV, SSM_STATE)), per_group((1, SSM_STATE)),
            per_group((SSM_CONV, SSM_STATE)), per_group((1, SSM_STATE)),
            per_group((LANE, LANE)), per_group((1, LANE)), per_group((1, LANE)),
            pl.BlockSpec((LANE, GROUP_WIDTH), lambda b, gi, c: (0, 0)),
            per_group((1, GROUP_WIDTH)), per_group((1, GROUP_WIDTH)),
        ],
        out_specs=pl.BlockSpec((CHUNK, GROUP_WIDTH), lambda b, gi, c: (rows(b, gi, c), gi)),
        out_shape=jax.ShapeDtypeStruct((t, D_INNER), BF16),
        scratch_shapes=[
            pltpu.VMEM((SSM_STATE, GROUP_WIDTH), F32),
            pltpu.VMEM((HALO, GROUP_WIDTH), F32),
            pltpu.VMEM((HALO, SSM_STATE), F32),
            pltpu.VMEM((HALO, SSM_STATE), F32),
        ],
        compiler_params=_cparams(("parallel", "parallel", "arbitrary")),
        name="ssd",
    )(pf, pf, pf, pf, dt_raw, cwx, cbx, cwb, cbb, cwc, cbc, sel, dtb, a_neg, eloc, dsk, nw)


def _attn_kernel(sink_ref, q_ref, kp_ref, kc_ref, vp_ref, vc_ref, o_ref):
    n = pl.program_id(1)
    qi = lax.broadcasted_iota(jnp.int32, (WINDOW, 2 * WINDOW), 0)
    kj = lax.broadcasted_iota(jnp.int32, (WINDOW, 2 * WINDOW), 1)
    dist = qi - kj + WINDOW
    valid = (dist >= 0) & (dist < WINDOW) & ((kj >= WINDOW) | (n > 0))
    distf = dist.astype(F32)
    scale = ATTN_HEAD_DIM ** -0.5
    neg_inf = jnp.float32(-jnp.inf)
    kcat = jnp.concatenate([kp_ref[...], kc_ref[...]], axis=0)
    vcat = jnp.concatenate([vp_ref[...], vc_ref[...]], axis=0)
    for kh in range(ATTN_KV_HEADS):
        k_h = kcat[:, kh * ATTN_HEAD_DIM:(kh + 1) * ATTN_HEAD_DIM]
        v_h = vcat[:, kh * ATTN_HEAD_DIM:(kh + 1) * ATTN_HEAD_DIM]
        for r in range(ATTN_GROUP):
            h = kh * ATTN_GROUP + r
            slope = 2.0 ** (-8.0 / ATTN_HEADS * (h + 1))
            q_h = q_ref[:, h * ATTN_HEAD_DIM:(h + 1) * ATTN_HEAD_DIM]
            logits = jnp.where(valid, _dot_nt(q_h, k_h) * scale - slope * distf, neg_inf)
            sink = sink_ref[h]
            m = jnp.maximum(jnp.max(logits, axis=-1, keepdims=True), sink)
            p = jnp.exp(logits - m)
            denom = jnp.sum(p, axis=-1, keepdims=True) + jnp.exp(sink - m)
            out = _dot(p.astype(BF16), v_h) / denom
            o_ref[:, h * ATTN_HEAD_DIM:(h + 1) * ATTN_HEAD_DIM] = out.astype(o_ref.dtype)


def _attention(qkv, sinks, batch, seq):
    t = qkv.shape[0]
    nb = seq // WINDOW
    k_blk = ATTN_WIDTH // KV_WIDTH
    v_blk = k_blk + 1

    def cur(b, n):
        return b * nb + n

    def prev(b, n):
        return b * nb + jnp.maximum(n - 1, 0)

    return pl.pallas_call(
        _attn_kernel,
        grid=(batch, nb),
        in_specs=[
            pl.BlockSpec(memory_space=pltpu.SMEM),
            pl.BlockSpec((WINDOW, ATTN_WIDTH), lambda b, n: (cur(b, n), 0)),
            pl.BlockSpec((WINDOW, KV_WIDTH), lambda b, n: (prev(b, n), k_blk)),
            pl.BlockSpec((WINDOW, KV_WIDTH), lambda b, n: (cur(b, n), k_blk)),
            pl.BlockSpec((WINDOW, KV_WIDTH), lambda b, n: (prev(b, n), v_blk)),
            pl.BlockSpec((WINDOW, KV_WIDTH), lambda b, n: (cur(b, n), v_blk)),
        ],
        out_specs=pl.BlockSpec((WINDOW, ATTN_WIDTH), lambda b, n: (cur(b, n), 0)),
        out_shape=jax.ShapeDtypeStruct((t, ATTN_WIDTH), BF16),
        compiler_params=_cparams(("parallel", "parallel")),
        name="swa",
    )(sinks, qkv, qkv, qkv, qkv, qkv)


def _merge_kernel(y_ref, a_ref, gs_ref, ga_ref, wso_ref, wao_ref, o_ref):
    y_ssm = _dot(y_ref[...], wso_ref[...])
    y_attn = _dot(a_ref[...], wao_ref[...])
    merged = jax.nn.sigmoid(gs_ref[...]) * y_ssm + jax.nn.sigmoid(ga_ref[...]) * y_attn
    o_ref[...] = merged.astype(o_ref.dtype)


def _merge(y, a, pf, wso, wao, tm, tn):
    t = y.shape[0]
    gs_blk0 = OFF_DT // tn
    ga_blk0 = (OFF_DT + D_MODEL) // tn
    return pl.pallas_call(
        _merge_kernel,
        grid=(D_MODEL // tn, t // tm),
        in_specs=[
            pl.BlockSpec((tm, D_INNER), lambda j, i: (i, 0)),
            pl.BlockSpec((tm, ATTN_WIDTH), lambda j, i: (i, 0)),
            pl.BlockSpec((tm, tn), lambda j, i: (i, gs_blk0 + j)),
            pl.BlockSpec((tm, tn), lambda j, i: (i, ga_blk0 + j)),
            pl.BlockSpec((D_INNER, tn), lambda j, i: (0, j)),
            pl.BlockSpec((ATTN_WIDTH, tn), lambda j, i: (0, j)),
        ],
        out_specs=pl.BlockSpec((tm, tn), lambda j, i: (i, j)),
        out_shape=jax.ShapeDtypeStruct((t, D_MODEL), BF16),
        compiler_params=_cparams(("parallel", "parallel")),
        name="merge",
    )(y, a, pf, pf, wso, wao)


def _oproj_kernel(m_ref, w_ref, x_ref, o_ref):
    o_ref[...] = x_ref[...] + _dot(m_ref[...], w_ref[...])


def _oproj(merged, wo, x2, tm):
    t = x2.shape[0]
    return pl.pallas_call(
        _oproj_kernel,
        grid=(t // tm,),
        in_specs=[
            pl.BlockSpec((tm, D_MODEL), lambda i: (i, 0)),
            pl.BlockSpec((D_MODEL, D_MODEL), lambda i: (0, 0)),
            pl.BlockSpec((tm, D_MODEL), lambda i: (i, 0)),
        ],
        out_specs=pl.BlockSpec((tm, D_MODEL), lambda i: (i, 0)),
        out_shape=jax.ShapeDtypeStruct((t, D_MODEL), F32),
        compiler_params=_cparams(("parallel",)),
        name="oproj",
    )(merged, wo, x2)


def _ffn_kernel(tiles_per_seq, h_ref, halo_ref, nw_ref, wg_ref, wv_ref, cw_ref, cb_ref, wout_ref,
                fw_ref, o_ref, u_scr, acc_scr):
    i = pl.program_id(0)
    j = pl.program_id(1)
    tm = h_ref.shape[0]

    @pl.when(j == 0)
    def _():
        nw = nw_ref[...]
        first = (i % tiles_per_seq) == 0
        halo = _rms_scale(halo_ref[...], nw)
        u_scr[0:FFN_HALO, :] = jnp.where(first, 0.0, halo).astype(BF16)
        u_scr[FFN_HALO:, :] = _rms_scale(h_ref[...], nw).astype(BF16)
        acc_scr[...] = jnp.zeros_like(acc_scr)

    gate = _dot(u_scr[...], wg_ref[...])
    val = _dot(u_scr[FFN_HALO:, :], wv_ref[...])
    cw = cw_ref[...]
    conv = cb_ref[...] + cw[FFN_CONV - 1:FFN_CONV] * gate[FFN_HALO:]
    for k in range(FFN_CONV - 1):
        lo = FFN_HALO - (FFN_CONV - 1) + k
        conv = conv + cw[k:k + 1] * gate[lo:lo + tm]
    act = 0.5 * conv * (1.0 + lax.erf(conv * (2.0 ** -0.5)))
    acc_scr[...] += _dot((act * val).astype(BF16), wout_ref[...])

    @pl.when(j == pl.num_programs(1) - 1)
    def _():
        o_ref[...] = _rms_scale(h_ref[...] + acc_scr[...], fw_ref[...])


def _ffn(h1, nw, wg, wv, cw, cb, wout, fw, seq, tm, tn):
    t = h1.shape[0]
    halo_blocks = tm // FFN_HALO
    return pl.pallas_call(
        functools.partial(_ffn_kernel, seq // tm),
        grid=(t // tm, D_FF // tn),
        in_specs=[
            pl.BlockSpec((tm, D_MODEL), lambda i, j: (i, 0)),
            pl.BlockSpec((FFN_HALO, D_MODEL), lambda i, j: (jnp.maximum(i * halo_blocks - 1, 0), 0)),
            pl.BlockSpec((1, D_MODEL), lambda i, j: (0, 0)),
            pl.BlockSpec((D_MODEL, tn), lambda i, j: (0, j)),
            pl.BlockSpec((D_MODEL, tn), lambda i, j: (0, j)),
            pl.BlockSpec((FFN_CONV, tn), lambda i, j: (0, j)),
            pl.BlockSpec((1, tn), lambda i, j: (0, j)),
            pl.BlockSpec((tn, D_MODEL), lambda i, j: (j, 0)),
            pl.BlockSpec((1, D_MODEL), lambda i, j: (0, 0)),
        ],
        out_specs=pl.BlockSpec((tm, D_MODEL), lambda i, j: (i, 0)),
        out_shape=jax.ShapeDtypeStruct((t, D_MODEL), F32),
        scratch_shapes=[
            pltpu.VMEM((FFN_HALO + tm, D_MODEL), BF16),
            pltpu.VMEM((tm, D_MODEL), F32),
        ],
        compiler_params=_cparams(("parallel", "arbitrary")),
        name="conv_ffn",
    )(h1, h1, nw, wg, wv, cw, cb, wout, fw)


def _head_select():
    sel = np.zeros((SSM_GROUPS, LANE, LANE), np.float32)
    for g in range(SSM_GROUPS):
        for r in range(HEADS_PER_GROUP):
            sel[g, g * HEADS_PER_GROUP + r, r] = 1.0
    return jnp.asarray(sel, BF16)


def _head_expand():
    e = np.zeros((LANE, GROUP_WIDTH), np.float32)
    for r in range(HEADS_PER_GROUP):
        e[r, r * SSM_HEAD_DIM:(r + 1) * SSM_HEAD_DIM] = 1.0
    return jnp.asarray(e, BF16)


def _per_group_heads(v):
    vg = v.astype(F32).reshape(SSM_GROUPS, 1, HEADS_PER_GROUP)
    return jnp.pad(vg, ((0, 0), (0, 0), (0, LANE - HEADS_PER_GROUP)))


def kernel(x, attn_norm_w, w_in, ssm_conv_w, ssm_conv_b, dt_bias, a_log, d_skip, ssm_norm_w, attn_sinks, w_ssm_out, w_attn_out, w_o, ffn_norm_w, w_ffn_in, ffn_conv_w, ffn_conv_b, w_ffn_out, final_norm_w):
    batch, seq, d = x.shape
    t = batch * seq
    h2 = x.reshape(t, d)
    assert w_in.shape[0] == 1, "single-layer problem: the final norm is fused into the FFN kernel"
    tm = min(1024, t)
    tm_half = min(512, t)
    for i in range(1):
        wi = w_in[i]
        w_pf = jnp.concatenate([wi[:, :OFF_DT], wi[:, OFF_GATE:]], axis=1).astype(BF16)
        w_qkv = wi[:, OFF_Q:OFF_GATE].astype(BF16)
        w_dt = jnp.pad(wi[:, OFF_DT:OFF_Q], ((0, 0), (0, LANE - SSM_HEADS))).astype(BF16)
        anw = attn_norm_w[i].reshape(1, d)

        pf = _norm_proj(h2, anw, w_pf, F32, tm, 1024)
        qkv = _norm_proj(h2, anw, w_qkv, BF16, tm, QKV_WIDTH // 2)
        dt_raw = _norm_proj(h2, anw, w_dt, F32, tm, LANE)

        cw = ssm_conv_w[i]
        cb = ssm_conv_b[i]
        g = SSM_GROUPS
        cwx = cw[:, :D_INNER].reshape(SSM_CONV, g, GROUP_WIDTH).transpose(1, 0, 2)
        cbx = cb[:D_INNER].reshape(g, 1, GROUP_WIDTH)
        cwb = cw[:, D_INNER:D_INNER + SSM_GN].reshape(SSM_CONV, g, SSM_STATE).transpose(1, 0, 2)
        cbb = cb[D_INNER:D_INNER + SSM_GN].reshape(g, 1, SSM_STATE)
        cwc = cw[:, D_INNER + SSM_GN:].reshape(SSM_CONV, g, SSM_STATE).transpose(1, 0, 2)
        cbc = cb[D_INNER + SSM_GN:].reshape(g, 1, SSM_STATE)
        dtb = _per_group_heads(dt_bias[i])
        a_neg = _per_group_heads(-jnp.exp(a_log[i].astype(F32)))
        dsk = jnp.repeat(d_skip[i].astype(F32), SSM_HEAD_DIM).reshape(g, 1, GROUP_WIDTH)
        snw = ssm_norm_w[i].astype(F32).reshape(g, 1, GROUP_WIDTH)
        y = _ssd(pf, dt_raw, cwx, cbx, cwb, cbb, cwc, cbc, _head_select(), dtb, a_neg,
                 _head_expand(), dsk, snw, batch, seq)

        attn = _attention(qkv, attn_sinks[i].astype(F32), batch, seq)

        merged = _merge(y, attn, pf, w_ssm_out[i].astype(BF16), w_attn_out[i].astype(BF16),
                        tm_half, 1024)
        h1 = _oproj(merged, w_o[i].astype(BF16), h2, tm_half)

        wf = w_ffn_in[i]
        fw = final_norm_w.reshape(1, d)
        h2 = _ffn(h1, ffn_norm_w[i].reshape(1, d), wf[:, :D_FF].astype(BF16), wf[:, D_FF:].astype(BF16),
                  ffn_conv_w[i], ffn_conv_b[i].reshape(1, D_FF), w_ffn_out[i].astype(BF16), fw,
                  seq, tm_half, 512)
    return h2.reshape(batch, seq, d)
```

```python
import functools
import math

import numpy as np
import jax
import jax.numpy as jnp
from jax import lax
from jax.experimental import pallas as pl
from jax.experimental.pallas import tpu as pltpu

F32 = jnp.float32
BF16 = jnp.bfloat16

D_MODEL = 2048
D_INNER = 4096
SSM_HEADS = 64
SSM_HEAD_DIM = 64
SSM_GROUPS = 8
SSM_STATE = 128
SSM_CONV = 4
CHUNK = 128
GROUP_WIDTH = D_INNER // SSM_GROUPS
HEADS_PER_GROUP = SSM_HEADS // SSM_GROUPS
SSM_GN = SSM_GROUPS * SSM_STATE
ATTN_HEADS = 32
ATTN_KV_HEADS = 4
ATTN_GROUP = ATTN_HEADS // ATTN_KV_HEADS
ATTN_HEAD_DIM = 64
ATTN_WIDTH = ATTN_HEADS * ATTN_HEAD_DIM
KV_WIDTH = ATTN_KV_HEADS * ATTN_HEAD_DIM
WINDOW = 128
D_FF = 5632
FFN_CONV = 3
NORM_EPS = 1e-5
LOG2E = math.log2(math.e)

W_OFF_XBC = D_INNER
W_OFF_DT = D_INNER + D_INNER + 2 * SSM_GN
W_OFF_Q = W_OFF_DT + SSM_HEADS
W_OFF_K = W_OFF_Q + ATTN_WIDTH
W_OFF_GATE = W_OFF_K + 2 * KV_WIDTH

P_TN = 1024
P_Z = 0
P_XS = D_INNER
P_B = 2 * D_INNER
P_C = P_B + SSM_GN
P_GATE = P_C + SSM_GN
P_Q = P_GATE + 2 * D_MODEL
P_K = P_Q + ATTN_WIDTH
P_V = P_K + KV_WIDTH
P_DT = P_V + KV_WIDTH
P_WIDTH = 17 * P_TN
P_CONV_BLK0 = P_XS // P_TN
P_CONV_BLK1 = P_GATE // P_TN
P_GATE_BLK1 = P_Q // P_TN

LANE = 128
HALO = 16
VMEM_LIMIT = 56 * 1024 * 1024


def _cparams(semantics):
    return pltpu.CompilerParams(dimension_semantics=semantics, vmem_limit_bytes=VMEM_LIMIT)


def _dot(a, b):
    return jnp.dot(a, b, preferred_element_type=F32)


def _dot_nt(a, b):
    return lax.dot_general(a, b, (((1,), (1,)), ((), ())), preferred_element_type=F32)


def _split3(v):
    hi = v.astype(BF16).astype(F32)
    r1 = v - hi
    mid = r1.astype(BF16).astype(F32)
    lo = (r1 - mid).astype(BF16).astype(F32)
    return hi, mid, lo


def _rms_scale(x, w):
    ms = jnp.mean(x * x, axis=-1, keepdims=True)
    return x * lax.rsqrt(ms + NORM_EPS) * w


def _silu(v):
    return v * jax.nn.sigmoid(v)


def _causal_conv(ext, w, b, taps, rows):
    acc = b + w[taps - 1:taps] * ext[HALO:HALO + rows]
    for k in range(taps - 1):
        lo = HALO - (taps - 1) + k
        acc = acc + w[k:k + 1] * ext[lo:lo + rows]
    return acc


def _in_proj_kernel(tiles_per_seq, x_ref, halo_ref, nw_ref, w_ref, cw_ref, cb_ref, o_ref, u_scr):
    i = pl.program_id(0)
    j = pl.program_id(1)
    tm = x_ref.shape[0]

    @pl.when(j == 0)
    def _():
        nw = nw_ref[...]
        first = (i % tiles_per_seq) == 0
        u_scr[0:HALO, :] = jnp.where(first, 0.0, _rms_scale(halo_ref[...], nw)).astype(BF16)
        u_scr[HALO:, :] = _rms_scale(x_ref[...], nw).astype(BF16)

    is_conv = (j >= P_CONV_BLK0) & (j < P_CONV_BLK1)

    @pl.when(is_conv)
    def _():
        ext = _dot(u_scr[...], w_ref[...])
        o_ref[...] = _silu(_causal_conv(ext, cw_ref[...], cb_ref[...], SSM_CONV, tm))

    @pl.when(jnp.logical_not(is_conv))
    def _():
        acc = _dot(u_scr[HALO:, :], w_ref[...])

        @pl.when(j < P_CONV_BLK0)
        def _():
            o_ref[...] = _silu(acc)

        @pl.when((j >= P_CONV_BLK1) & (j < P_GATE_BLK1))
        def _():
            o_ref[...] = jax.nn.sigmoid(acc)

        @pl.when(j >= P_GATE_BLK1)
        def _():
            o_ref[...] = acc


def _in_proj(x2, nw, w, cw, cb, seq, tm):
    t, d = x2.shape
    halo_blocks = tm // HALO
    return pl.pallas_call(
        functools.partial(_in_proj_kernel, seq // tm),
        grid=(t // tm, P_WIDTH // P_TN),
        in_specs=[
            pl.BlockSpec((tm, d), lambda i, j: (i, 0)),
            pl.BlockSpec((HALO, d), lambda i, j: (jnp.maximum(i * halo_blocks - 1, 0), 0)),
            pl.BlockSpec((1, d), lambda i, j: (0, 0)),
            pl.BlockSpec((d, P_TN), lambda i, j: (0, j)),
            pl.BlockSpec((SSM_CONV, P_TN), lambda i, j: (0, j)),
            pl.BlockSpec((1, P_TN), lambda i, j: (0, j)),
        ],
        out_specs=pl.BlockSpec((tm, P_TN), lambda i, j: (i, j)),
        out_shape=jax.ShapeDtypeStruct((t, P_WIDTH), F32),
        scratch_shapes=[pltpu.VMEM((HALO + tm, d), BF16)],
        compiler_params=_cparams(("parallel", "arbitrary")),
        name="in_proj",
    )(x2, x2, nw, w, cw, cb)


def _ssd_kernel(sz_ref, xs_ref, bc_ref, dt_ref, dtb_ref, a_ref, epk_ref, dsk_ref, nw_ref,
                y_ref, state_scr):
    @pl.when(pl.program_id(1) == 0)
    def _():
        state_scr[...] = jnp.zeros_like(state_scr)

    v = dt_ref[...] + dtb_ref[...]
    dt = jnp.maximum(v, 0.0) + jnp.log1p(jnp.exp(-jnp.abs(v)))
    ad = dt * a_ref[...]

    row = lax.broadcasted_iota(jnp.int32, (CHUNK, CHUNK), 0)
    col = lax.broadcasted_iota(jnp.int32, (CHUNK, CHUNK), 1)
    causal = row >= col
    tril = causal.astype(BF16)
    a_cum = sum(_dot(tril, p.astype(BF16)) for p in _split3(ad))
    a_cum_t = a_cum.T
    ea = jnp.exp(a_cum)
    ds = jnp.exp(a_cum[CHUNK - 1:CHUNK, :] - a_cum)

    low_half = col < SSM_HEADS

    def packed(q):
        hi, mid, lo = _split3(q)
        first = jnp.where(low_half, hi, pltpu.roll(mid, SSM_HEADS, axis=1))
        return jnp.concatenate([first, lo], axis=1).astype(BF16)

    stack = jnp.concatenate([packed(dt), packed(ea), packed(ds)], axis=0)

    lane_low = lax.broadcasted_iota(jnp.int32, (CHUNK, LANE), 1) < SSM_HEAD_DIM
    neg_inf = jnp.float32(-jnp.inf)
    for g in range(SSM_GROUPS):
        c0 = g * GROUP_WIDTH
        expanded = _dot(stack, epk_ref[:, c0:c0 + GROUP_WIDTH])
        dtx = expanded[0:CHUNK]
        eax = expanded[CHUNK:2 * CHUNK]
        dsx = expanded[2 * CHUNK:3 * CHUNK]
        xs = xs_ref[:, c0:c0 + GROUP_WIDTH]
        bm = bc_ref[:, g * SSM_STATE:(g + 1) * SSM_STATE]
        cm16 = bc_ref[:, SSM_GN + g * SSM_STATE:SSM_GN + (g + 1) * SSM_STATE].astype(BF16)
        xdt = xs * dtx
        cb = _dot_nt(cm16, bm.astype(BF16))

        y_parts = []
        for pr in range(HEADS_PER_GROUP // 2):
            ms = []
            for h in (g * HEADS_PER_GROUP + 2 * pr, g * HEADS_PER_GROUP + 2 * pr + 1):
                seg = a_cum[:, h:h + 1] - a_cum_t[h:h + 1, :]
                decay = jnp.exp(jnp.where(causal, seg, neg_inf))
                ms.append((cb * decay).astype(BF16))
            m_pair = jnp.concatenate(ms, axis=1)
            xp = xdt[:, pr * LANE:(pr + 1) * LANE]
            x_pair = jnp.concatenate([jnp.where(lane_low, xp, 0.0), jnp.where(lane_low, 0.0, xp)],
                                     axis=0).astype(BF16)
            y_parts.append(_dot(m_pair, x_pair))
        y = jnp.concatenate(y_parts, axis=1)

        state = state_scr[g]
        y = y + _dot(cm16, state.astype(BF16)) * eax
        new_state = _dot(bm.T.astype(BF16), (xdt * dsx).astype(BF16))
        state_scr[g] = state * eax[CHUNK - 1:CHUNK, :] + new_state

        y = (y + dsk_ref[:, c0:c0 + GROUP_WIDTH] * xs) * sz_ref[:, c0:c0 + GROUP_WIDTH]
        y_ref[:, c0:c0 + GROUP_WIDTH] = _rms_scale(y, nw_ref[:, c0:c0 + GROUP_WIDTH]).astype(y_ref.dtype)


def _ssd(p, dtb, a_neg, epk, dsk, nw, batch, seq):
    t = p.shape[0]
    nc = seq // CHUNK

    def rows(b, c):
        return b * nc + c

    const = lambda shape: pl.BlockSpec(shape, lambda b, c: (0, 0))
    return pl.pallas_call(
        _ssd_kernel,
        grid=(batch, nc),
        in_specs=[
            pl.BlockSpec((CHUNK, D_INNER), lambda b, c: (rows(b, c), P_Z // D_INNER)),
            pl.BlockSpec((CHUNK, D_INNER), lambda b, c: (rows(b, c), P_XS // D_INNER)),
            pl.BlockSpec((CHUNK, 2 * SSM_GN), lambda b, c: (rows(b, c), P_B // (2 * SSM_GN))),
            pl.BlockSpec((CHUNK, LANE), lambda b, c: (rows(b, c), P_DT // LANE)),
            const((1, LANE)), const((1, LANE)),
            const((2 * LANE, D_INNER)),
            const((1, D_INNER)), const((1, D_INNER)),
        ],
        out_specs=pl.BlockSpec((CHUNK, D_INNER), lambda b, c: (rows(b, c), 0)),
        out_shape=jax.ShapeDtypeStruct((t, D_INNER), BF16),
        scratch_shapes=[pltpu.VMEM((SSM_GROUPS, SSM_STATE, GROUP_WIDTH), F32)],
        compiler_params=_cparams(("parallel", "arbitrary")),
        name="ssd",
    )(p, p, p, p, dtb, a_neg, epk, dsk, nw)


def _attn_kernel(sink_ref, q_ref, kp_ref, kc_ref, vp_ref, vc_ref, bias_ref, o_ref):
    c = (ATTN_HEAD_DIM ** -0.5) * LOG2E
    kcat = jnp.concatenate([kp_ref[...], kc_ref[...]], axis=0).astype(BF16)
    v_t = jnp.concatenate([vp_ref[...], vc_ref[...]], axis=0).T.astype(BF16)
    q_t = q_ref[...].T.astype(BF16)
    for kh in range(ATTN_KV_HEADS):
        d0 = kh * ATTN_HEAD_DIM
        k_h = kcat[:, d0:d0 + ATTN_HEAD_DIM]
        heads = [kh * ATTN_GROUP + r for r in range(ATTN_GROUP)]
        q_g = jnp.concatenate([q_t[h * ATTN_HEAD_DIM:(h + 1) * ATTN_HEAD_DIM, :] for h in heads],
                              axis=1)
        s_t = _dot(k_h, q_g)
        ps, dens = [], []
        for r, h in enumerate(heads):
            l2 = s_t[:, r * WINDOW:(r + 1) * WINDOW] * c + bias_ref[0, h]
            sink2 = sink_ref[h] * LOG2E
            m = jnp.maximum(jnp.max(l2, axis=0, keepdims=True), sink2)
            p = jnp.exp2(l2 - m)
            dens.append(jnp.sum(p, axis=0, keepdims=True) + jnp.exp2(sink2 - m))
            ps.append(p.astype(BF16))
        p_t = jnp.concatenate(ps, axis=1)
        o_t = _dot(v_t[d0:d0 + ATTN_HEAD_DIM, :], p_t)
        o_g = jnp.concatenate([o_t[:, r * WINDOW:(r + 1) * WINDOW] / dens[r]
                               for r in range(ATTN_GROUP)], axis=0)
        w0 = kh * ATTN_GROUP * ATTN_HEAD_DIM
        o_ref[:, w0:w0 + ATTN_GROUP * ATTN_HEAD_DIM] = o_g.T.astype(o_ref.dtype)


def _alibi_bias_tables():
    j = jnp.arange(2 * WINDOW)[:, None]
    l = jnp.arange(WINDOW)[None, :]
    dist = l - j + WINDOW
    band = (dist >= 0) & (dist < WINDOW)
    valid = jnp.stack([band & (j >= WINDOW), band])
    slopes = jnp.exp2(-8.0 / ATTN_HEADS * jnp.arange(1, ATTN_HEADS + 1, dtype=F32))
    bias = -slopes[:, None, None] * dist.astype(F32) * LOG2E
    return jnp.where(valid[:, None], bias[None], -jnp.inf)


def _attention(p, sinks, batch, seq):
    t = p.shape[0]
    nb = seq // WINDOW
    q_blk = P_Q // ATTN_WIDTH
    k_blk = P_K // KV_WIDTH
    v_blk = P_V // KV_WIDTH

    def cur(b, n):
        return b * nb + n

    def prev(b, n):
        return b * nb + jnp.maximum(n - 1, 0)

    return pl.pallas_call(
        _attn_kernel,
        grid=(batch, nb),
        in_specs=[
            pl.BlockSpec(memory_space=pltpu.SMEM),
            pl.BlockSpec((WINDOW, ATTN_WIDTH), lambda b, n: (cur(b, n), q_blk)),
            pl.BlockSpec((WINDOW, KV_WIDTH), lambda b, n: (prev(b, n), k_blk)),
            pl.BlockSpec((WINDOW, KV_WIDTH), lambda b, n: (cur(b, n), k_blk)),
            pl.BlockSpec((WINDOW, KV_WIDTH), lambda b, n: (prev(b, n), v_blk)),
            pl.BlockSpec((WINDOW, KV_WIDTH), lambda b, n: (cur(b, n), v_blk)),
            pl.BlockSpec((1, ATTN_HEADS, 2 * WINDOW, WINDOW), lambda b, n: (jnp.minimum(n, 1), 0, 0, 0)),
        ],
        out_specs=pl.BlockSpec((WINDOW, ATTN_WIDTH), lambda b, n: (cur(b, n), 0)),
        out_shape=jax.ShapeDtypeStruct((t, ATTN_WIDTH), BF16),
        compiler_params=_cparams(("parallel", "parallel")),
        name="swa",
    )(sinks, p, p, p, p, p, _alibi_bias_tables())


def _merge_kernel(y_ref, a_ref, gs_ref, ga_ref, wso_ref, wao_ref, o_ref):
    y_ssm = _dot(y_ref[...], wso_ref[...])
    y_attn = _dot(a_ref[...], wao_ref[...])
    o_ref[...] = (gs_ref[...] * y_ssm + ga_ref[...] * y_attn).astype(o_ref.dtype)


def _merge(y, a, p, wso, wao, tm, tn):
    t = y.shape[0]
    gs_blk0 = P_GATE // tn
    ga_blk0 = (P_GATE + D_MODEL) // tn
    return pl.pallas_call(
        _merge_kernel,
        grid=(D_MODEL // tn, t // tm),
        in_specs=[
            pl.BlockSpec((tm, D_INNER), lambda j, i: (i, 0)),
            pl.BlockSpec((tm, ATTN_WIDTH), lambda j, i: (i, 0)),
            pl.BlockSpec((tm, tn), lambda j, i: (i, gs_blk0 + j)),
            pl.BlockSpec((tm, tn), lambda j, i: (i, ga_blk0 + j)),
            pl.BlockSpec((D_INNER, tn), lambda j, i: (0, j)),
            pl.BlockSpec((ATTN_WIDTH, tn), lambda j, i: (0, j)),
        ],
        out_specs=pl.BlockSpec((tm, tn), lambda j, i: (i, j)),
        out_shape=jax.ShapeDtypeStruct((t, D_MODEL), BF16),
        compiler_params=_cparams(("parallel", "parallel")),
        name="merge",
    )(y, a, p, p, wso, wao)


def _oproj_kernel(m_ref, w_ref, x_ref, o_ref):
    o_ref[...] = x_ref[...] + _dot(m_ref[...], w_ref[...])


def _oproj(merged, wo, x2, tm):
    t = x2.shape[0]
    return pl.pallas_call(
        _oproj_kernel,
        grid=(t // tm,),
        in_specs=[
            pl.BlockSpec((tm, D_MODEL), lambda i: (i, 0)),
            pl.BlockSpec((D_MODEL, D_MODEL), lambda i: (0, 0)),
            pl.BlockSpec((tm, D_MODEL), lambda i: (i, 0)),
        ],
        out_specs=pl.BlockSpec((tm, D_MODEL), lambda i: (i, 0)),
        out_shape=jax.ShapeDtypeStruct((t, D_MODEL), F32),
        compiler_params=_cparams(("parallel",)),
        name="oproj",
    )(merged, wo, x2)


def _ffn_kernel(tiles_per_seq, h_ref, halo_ref, nw_ref, wg_ref, wv_ref, cw_ref, cb_ref, wout_ref,
                fw_ref, o_ref, u_scr, acc_scr):
    i = pl.program_id(0)
    j = pl.program_id(1)
    tm = h_ref.shape[0]

    @pl.when(j == 0)
    def _():
        nw = nw_ref[...]
        first = (i % tiles_per_seq) == 0
        u_scr[0:HALO, :] = jnp.where(first, 0.0, _rms_scale(halo_ref[...], nw)).astype(BF16)
        u_scr[HALO:, :] = _rms_scale(h_ref[...], nw).astype(BF16)
        acc_scr[...] = jnp.zeros_like(acc_scr)

    gate = _dot(u_scr[...], wg_ref[...])
    val = _dot(u_scr[HALO:, :], wv_ref[...])
    conv = _causal_conv(gate, cw_ref[...], cb_ref[...], FFN_CONV, tm)
    act = 0.5 * conv * (1.0 + lax.erf(conv * (2.0 ** -0.5)))
    acc_scr[...] += _dot((act * val).astype(BF16), wout_ref[...])

    @pl.when(j == pl.num_programs(1) - 1)
    def _():
        o_ref[...] = _rms_scale(h_ref[...] + acc_scr[...], fw_ref[...])


def _ffn(h1, nw, wg, wv, cw, cb, wout, fw, seq, tm, tn):
    t = h1.shape[0]
    halo_blocks = tm // HALO
    return pl.pallas_call(
        functools.partial(_ffn_kernel, seq // tm),
        grid=(t // tm, D_FF // tn),
        in_specs=[
            pl.BlockSpec((tm, D_MODEL), lambda i, j: (i, 0)),
            pl.BlockSpec((HALO, D_MODEL), lambda i, j: (jnp.maximum(i * halo_blocks - 1, 0), 0)),
            pl.BlockSpec((1, D_MODEL), lambda i, j: (0, 0)),
            pl.BlockSpec((D_MODEL, tn), lambda i, j: (0, j)),
            pl.BlockSpec((D_MODEL, tn), lambda i, j: (0, j)),
            pl.BlockSpec((FFN_CONV, tn), lambda i, j: (0, j)),
            pl.BlockSpec((1, tn), lambda i, j: (0, j)),
            pl.BlockSpec((tn, D_MODEL), lambda i, j: (j, 0)),
            pl.BlockSpec((1, D_MODEL), lambda i, j: (0, 0)),
        ],
        out_specs=pl.BlockSpec((tm, D_MODEL), lambda i, j: (i, 0)),
        out_shape=jax.ShapeDtypeStruct((t, D_MODEL), F32),
        scratch_shapes=[
            pltpu.VMEM((HALO + tm, D_MODEL), BF16),
            pltpu.VMEM((tm, D_MODEL), F32),
        ],
        compiler_params=_cparams(("parallel", "arbitrary")),
        name="conv_ffn",
    )(h1, h1, nw, wg, wv, cw, cb, wout, fw)


def _packed_head_expand():
    e = np.zeros((2 * LANE, D_INNER), np.float32)
    for part in range(3):
        for h in range(SSM_HEADS):
            e[part * SSM_HEADS + h, h * SSM_HEAD_DIM:(h + 1) * SSM_HEAD_DIM] = 1.0
    return jnp.asarray(e, BF16)


def _pad_lanes(v, width):
    v = v.astype(F32).reshape(1, -1)
    return jnp.pad(v, ((0, 0), (0, width - v.shape[1])))


def kernel(x, attn_norm_w, w_in, ssm_conv_w, ssm_conv_b, dt_bias, a_log, d_skip, ssm_norm_w, attn_sinks, w_ssm_out, w_attn_out, w_o, ffn_norm_w, w_ffn_in, ffn_conv_w, ffn_conv_b, w_ffn_out, final_norm_w):
    batch, seq, d = x.shape
    t = batch * seq
    assert w_in.shape[0] == 1, "single-layer problem: the final norm is fused into the FFN kernel"
    tm = min(1024, seq)
    tm_half = min(512, seq)
    assert seq % tm == 0 and seq % CHUNK == 0
    x2 = x.reshape(t, d)

    wi = w_in[0]
    pad_cols = P_WIDTH - P_DT - SSM_HEADS
    w_p = jnp.concatenate(
        [wi[:, :W_OFF_DT], wi[:, W_OFF_GATE:], wi[:, W_OFF_Q:W_OFF_GATE], wi[:, W_OFF_DT:W_OFF_Q],
         jnp.zeros((d, pad_cols), wi.dtype)], axis=1).astype(BF16)
    conv_cols = ((0, 0), (P_XS, P_WIDTH - P_GATE))
    cw_p = jnp.pad(ssm_conv_w[0].astype(F32), conv_cols)
    cb_p = jnp.pad(ssm_conv_b[0].astype(F32).reshape(1, -1), conv_cols)
    p = _in_proj(x2, attn_norm_w[0].reshape(1, d), w_p, cw_p, cb_p, seq, tm)

    dtb = _pad_lanes(dt_bias[0], LANE)
    a_neg = _pad_lanes(-jnp.exp(a_log[0].astype(F32)), LANE)
    dsk = jnp.repeat(d_skip[0].astype(F32), SSM_HEAD_DIM).reshape(1, D_INNER)
    snw = ssm_norm_w[0].astype(F32).reshape(1, D_INNER)
    y = _ssd(p, dtb, a_neg, _packed_head_expand(), dsk, snw, batch, seq)

    attn = _attention(p, attn_sinks[0].astype(F32), batch, seq)

    merged = _merge(y, attn, p, w_ssm_out[0].astype(BF16), w_attn_out[0].astype(BF16), tm_half, 1024)
    h1 = _oproj(merged, w_o[0].astype(BF16), x2, tm_half)

    wf = w_ffn_in[0]
    out = _ffn(h1, ffn_norm_w[0].reshape(1, d), wf[:, :D_FF].astype(BF16), wf[:, D_FF:].astype(BF16),
               ffn_conv_w[0], ffn_conv_b[0].reshape(1, D_FF), w_ffn_out[0].astype(BF16),
               final_norm_w.reshape(1, d), seq, tm_half, 512)
    return out.reshape(batch, seq, d)
```

```python
import functools
import math

import numpy as np
import jax
import jax.numpy as jnp
from jax import lax
from jax.experimental import pallas as pl
from jax.experimental.pallas import tpu as pltpu

F32 = jnp.float32
BF16 = jnp.bfloat16

D_MODEL = 2048
D_INNER = 4096
SSM_HEADS = 64
SSM_HEAD_DIM = 64
SSM_GROUPS = 8
SSM_STATE = 128
SSM_CONV = 4
CHUNK = 128
GROUP_WIDTH = D_INNER // SSM_GROUPS
HEADS_PER_GROUP = SSM_HEADS // SSM_GROUPS
SSM_GN = SSM_GROUPS * SSM_STATE
ATTN_HEADS = 32
ATTN_KV_HEADS = 4
ATTN_GROUP = ATTN_HEADS // ATTN_KV_HEADS
ATTN_HEAD_DIM = 64
ATTN_WIDTH = ATTN_HEADS * ATTN_HEAD_DIM
KV_WIDTH = ATTN_KV_HEADS * ATTN_HEAD_DIM
WINDOW = 128
D_FF = 5632
FFN_CONV = 3
NORM_EPS = 1e-5
LOG2E = math.log2(math.e)

P_Z = 0
P_XS = D_INNER
P_B = 2 * D_INNER
P_C = P_B + SSM_GN
P_DT = P_C + SSM_GN
P_Q = P_DT + SSM_HEADS
P_K = P_Q + ATTN_WIDTH
P_V = P_K + KV_WIDTH
P_GATE = P_V + KV_WIDTH
P_TN = 1024
P_SUB = 256
P_CONV_BLK0 = P_XS // P_TN
P_CONV_BLK1 = P_DT // P_TN
P_RAW_BLK1 = pl.cdiv(P_GATE, P_TN)
P_SGATE = P_RAW_BLK1 * P_TN
P_WIDTH = P_SGATE + 2 * D_MODEL

LANE = 128
HALO = 16
SHIFT = P_Q % LANE
VMEM_LIMIT = 56 * 1024 * 1024


def _cparams(semantics):
    return pltpu.CompilerParams(dimension_semantics=semantics, vmem_limit_bytes=VMEM_LIMIT)


def _dot(a, b):
    return jnp.dot(a, b, preferred_element_type=F32)


def _dot_nt(a, b):
    return lax.dot_general(a, b, (((1,), (1,)), ((), ())), preferred_element_type=F32)


def _split3(v):
    hi = v.astype(BF16).astype(F32)
    r1 = v - hi
    mid = r1.astype(BF16).astype(F32)
    lo = (r1 - mid).astype(BF16).astype(F32)
    return hi, mid, lo


def _rms_scale(x, w):
    ms = jnp.mean(x * x, axis=-1, keepdims=True)
    return x * lax.rsqrt(ms + NORM_EPS) * w


def _sigmoid(v):
    return 0.5 * jnp.tanh(0.5 * v) + 0.5


def _silu(v):
    return v * _sigmoid(v)


def _causal_conv(ext, w, b, taps, rows):
    acc = b + w[taps - 1:taps] * ext[HALO:HALO + rows]
    for k in range(taps - 1):
        lo = HALO - (taps - 1) + k
        acc = acc + w[k:k + 1] * ext[lo:lo + rows]
    return acc


def _norm_kernel(x_ref, w_ref, o_ref):
    o_ref[...] = _rms_scale(x_ref[...], w_ref[...]).astype(o_ref.dtype)


def _norm(x2, w, tm):
    t, d = x2.shape
    return pl.pallas_call(
        _norm_kernel,
        grid=(t // tm,),
        in_specs=[pl.BlockSpec((tm, d), lambda i: (i, 0)), pl.BlockSpec((1, d), lambda i: (0, 0))],
        out_specs=pl.BlockSpec((tm, d), lambda i: (i, 0)),
        out_shape=jax.ShapeDtypeStruct((t, d), BF16),
        compiler_params=_cparams(("parallel",)),
        name="norm",
    )(x2, w)


def _in_proj_kernel(tiles_per_seq, u_ref, halo_ref, w_ref, cw_ref, cb_ref, o_ref, w_scr, lhs_scr):
    j = pl.program_id(0)
    i = pl.program_id(1)
    tm = u_ref.shape[0]
    sub = [slice(s * P_SUB, (s + 1) * P_SUB) for s in range(P_TN // P_SUB)]

    @pl.when(i == 0)
    def _():
        w_scr[...] = w_ref[...].T.astype(BF16)

    def direct(epilogue):
        for cols in sub:
            o_ref[:, cols] = epilogue(_dot(u_ref[...], w_scr[:, cols]))

    @pl.when(j < P_CONV_BLK0)
    def _():
        direct(_silu)

    @pl.when((j >= P_CONV_BLK1) & (j < P_RAW_BLK1))
    def _():
        direct(lambda acc: acc)

    @pl.when(j >= P_RAW_BLK1)
    def _():
        direct(_sigmoid)

    @pl.when((j >= P_CONV_BLK0) & (j < P_CONV_BLK1))
    def _():
        first = (i % tiles_per_seq) == 0
        halo = halo_ref[...]
        lhs_scr[0:HALO, :] = jnp.where(first, jnp.zeros_like(halo), halo)
        lhs_scr[HALO:, :] = u_ref[...]
        for cols in sub:
            ext = _dot(lhs_scr[...], w_scr[:, cols])
            o_ref[:, cols] = _silu(_causal_conv(ext, cw_ref[:, cols], cb_ref[:, cols], SSM_CONV, tm))


def _in_proj(u, w_in_t, cw, cb, seq, tm):
    t, d = u.shape
    halo_blocks = tm // HALO
    conv_blk = lambda j: jnp.clip(j - P_CONV_BLK0, 0, P_CONV_BLK1 - P_CONV_BLK0 - 1)
    w_row = lambda j: pl.multiple_of(
        jnp.where(j < P_RAW_BLK1, j * P_TN, P_GATE + (j - P_RAW_BLK1) * P_TN), 64)
    return pl.pallas_call(
        functools.partial(_in_proj_kernel, seq // tm),
        grid=(P_WIDTH // P_TN, t // tm),
        in_specs=[
            pl.BlockSpec((tm, d), lambda j, i: (i, 0)),
            pl.BlockSpec((HALO, d), lambda j, i: (jnp.maximum(i * halo_blocks - 1, 0), 0)),
            pl.BlockSpec((pl.Element(P_TN), pl.Element(d)), lambda j, i: (w_row(j), 0)),
            pl.BlockSpec((None, SSM_CONV, P_TN), lambda j, i: (0, 0, conv_blk(j))),
            pl.BlockSpec((1, P_TN), lambda j, i: (0, conv_blk(j))),
        ],
        out_specs=pl.BlockSpec((tm, P_TN), lambda j, i: (i, j)),
        out_shape=jax.ShapeDtypeStruct((t, P_WIDTH), F32),
        scratch_shapes=[pltpu.VMEM((d, P_TN), BF16), pltpu.VMEM((HALO + tm, d), BF16)],
        compiler_params=_cparams(("parallel", "arbitrary")),
        name="in_proj",
    )(u, u, w_in_t, cw, cb)


def _ssd_kernel(sz_ref, xs_ref, bc_ref, dt_ref, dtb_ref, a_ref, epk_ref, dsk_ref, nw_ref,
                y_ref, state_scr):
    @pl.when(pl.program_id(1) == 0)
    def _():
        state_scr[...] = jnp.zeros_like(state_scr)

    v = dt_ref[...] + dtb_ref[...]
    dt = jnp.maximum(v, 0.0) + jnp.log1p(jnp.exp(-jnp.abs(v)))
    ad = dt * a_ref[...]

    row = lax.broadcasted_iota(jnp.int32, (CHUNK, CHUNK), 0)
    col = lax.broadcasted_iota(jnp.int32, (CHUNK, CHUNK), 1)
    causal = row >= col
    tril = causal.astype(BF16)
    a_cum = sum(_dot(tril, p.astype(BF16)) for p in _split3(ad))
    a_cum_t = a_cum.T
    ea = jnp.exp(a_cum)
    ds = jnp.exp(a_cum[CHUNK - 1:CHUNK, :] - a_cum)

    low_half = col < SSM_HEADS

    def packed(q):
        hi, mid, lo = _split3(q)
        first = jnp.where(low_half, hi, pltpu.roll(mid, SSM_HEADS, axis=1))
        return jnp.concatenate([first, lo], axis=1).astype(BF16)

    stack = jnp.concatenate([packed(dt), packed(ea), packed(ds)], axis=0)

    lane_low = lax.broadcasted_iota(jnp.int32, (CHUNK, LANE), 1) < SSM_HEAD_DIM
    neg_inf = jnp.float32(-jnp.inf)
    for g in range(SSM_GROUPS):
        c0 = g * GROUP_WIDTH
        expanded = _dot(stack, epk_ref[:, c0:c0 + GROUP_WIDTH])
        dtx = expanded[0:CHUNK]
        eax = expanded[CHUNK:2 * CHUNK]
        dsx = expanded[2 * CHUNK:3 * CHUNK]
        xs = xs_ref[:, c0:c0 + GROUP_WIDTH]
        bm = bc_ref[:, g * SSM_STATE:(g + 1) * SSM_STATE]
        cm16 = bc_ref[:, SSM_GN + g * SSM_STATE:SSM_GN + (g + 1) * SSM_STATE].astype(BF16)
        xdt = xs * dtx
        cb = _dot_nt(cm16, bm.astype(BF16))

        y_parts = []
        for pr in range(HEADS_PER_GROUP // 2):
            ms = []
            for h in (g * HEADS_PER_GROUP + 2 * pr, g * HEADS_PER_GROUP + 2 * pr + 1):
                seg = a_cum[:, h:h + 1] - a_cum_t[h:h + 1, :]
                decay = jnp.exp(jnp.where(causal, seg, neg_inf))
                ms.append((cb * decay).astype(BF16))
            m_pair = jnp.concatenate(ms, axis=1)
            xp = xdt[:, pr * LANE:(pr + 1) * LANE]
            x_pair = jnp.concatenate([jnp.where(lane_low, xp, 0.0), jnp.where(lane_low, 0.0, xp)],
                                     axis=0).astype(BF16)
            y_parts.append(_dot(m_pair, x_pair))
        y = jnp.concatenate(y_parts, axis=1)

        state = state_scr[g]
        y = y + _dot(cm16, state.astype(BF16)) * eax
        new_state = _dot(bm.T.astype(BF16), (xdt * dsx).astype(BF16))
        state_scr[g] = state * eax[CHUNK - 1:CHUNK, :] + new_state

        y = (y + dsk_ref[:, c0:c0 + GROUP_WIDTH] * xs) * sz_ref[:, c0:c0 + GROUP_WIDTH]
        y_ref[:, c0:c0 + GROUP_WIDTH] = _rms_scale(y, nw_ref[:, c0:c0 + GROUP_WIDTH]).astype(y_ref.dtype)


def _ssd(p, dtb, a_neg, epk, dsk, nw, batch, seq):
    t = p.shape[0]
    nc = seq // CHUNK

    def rows(b, c):
        return b * nc + c

    const = lambda shape: pl.BlockSpec(shape, lambda b, c: (0, 0))
    return pl.pallas_call(
        _ssd_kernel,
        grid=(batch, nc),
        in_specs=[
            pl.BlockSpec((CHUNK, D_INNER), lambda b, c: (rows(b, c), P_Z // D_INNER)),
            pl.BlockSpec((CHUNK, D_INNER), lambda b, c: (rows(b, c), P_XS // D_INNER)),
            pl.BlockSpec((CHUNK, 2 * SSM_GN), lambda b, c: (rows(b, c), P_B // (2 * SSM_GN))),
            pl.BlockSpec((CHUNK, LANE), lambda b, c: (rows(b, c), P_DT // LANE)),
            const((1, LANE)), const((1, LANE)),
            const((2 * LANE, D_INNER)),
            const((1, D_INNER)), const((1, D_INNER)),
        ],
        out_specs=pl.BlockSpec((CHUNK, D_INNER), lambda b, c: (rows(b, c), 0)),
        out_shape=jax.ShapeDtypeStruct((t, D_INNER), BF16),
        scratch_shapes=[pltpu.VMEM((SSM_GROUPS, SSM_STATE, GROUP_WIDTH), F32)],
        compiler_params=_cparams(("parallel", "arbitrary")),
        name="ssd",
    )(p, p, p, p, dtb, a_neg, epk, dsk, nw)


Q_WIN0 = P_Q - SHIFT
KV_WIN0 = (P_K // P_TN) * P_TN
K_LOC = P_K - KV_WIN0
V_LOC = P_V - KV_WIN0
V_SLAB0 = (V_LOC // LANE) * LANE
V_SLAB1 = V_SLAB0 + KV_WIDTH + LANE


def _attn_kernel(sink_ref, q_ref, qx_ref, kvp_ref, kvc_ref, bias_ref, o_ref):
    c = (ATTN_HEAD_DIM ** -0.5) * LOG2E
    kv = jnp.concatenate([kvp_ref[...], kvc_ref[...]], axis=0)
    v_t = kv[:, V_SLAB0:V_SLAB1].T.astype(BF16)
    q_t = jnp.concatenate([q_ref[...].T, qx_ref[...].T], axis=0).astype(BF16)
    for kh in range(ATTN_KV_HEADS):
        d0 = kh * ATTN_HEAD_DIM
        k_h = kv[:, K_LOC + d0:K_LOC + d0 + ATTN_HEAD_DIM].astype(BF16)
        heads = [kh * ATTN_GROUP + r for r in range(ATTN_GROUP)]
        q_g = jnp.concatenate(
            [q_t[SHIFT + h * ATTN_HEAD_DIM:SHIFT + (h + 1) * ATTN_HEAD_DIM, :] for h in heads],
            axis=1)
        s_t = _dot(k_h, q_g)
        ps, dens = [], []
        for r, h in enumerate(heads):
            l2 = s_t[:, r * WINDOW:(r + 1) * WINDOW] * c + bias_ref[0, h]
            sink2 = sink_ref[h] * LOG2E
            m = jnp.maximum(jnp.max(l2, axis=0, keepdims=True), sink2)
            p = jnp.exp2(l2 - m)
            dens.append(jnp.sum(p, axis=0, keepdims=True) + jnp.exp2(sink2 - m))
            ps.append(p.astype(BF16))
        p_t = jnp.concatenate(ps, axis=1)
        v0 = V_LOC - V_SLAB0 + d0
        o_t = _dot(v_t[v0:v0 + ATTN_HEAD_DIM, :], p_t)
        o_g = jnp.concatenate([o_t[:, r * WINDOW:(r + 1) * WINDOW] / dens[r]
                               for r in range(ATTN_GROUP)], axis=0)
        w0 = kh * ATTN_GROUP * ATTN_HEAD_DIM
        o_ref[:, w0:w0 + ATTN_GROUP * ATTN_HEAD_DIM] = o_g.T.astype(o_ref.dtype)


def _alibi_bias_tables():
    j = jnp.arange(2 * WINDOW)[:, None]
    l = jnp.arange(WINDOW)[None, :]
    dist = l - j + WINDOW
    band = (dist >= 0) & (dist < WINDOW)
    valid = jnp.stack([band & (j >= WINDOW), band])
    slopes = jnp.exp2(-8.0 / ATTN_HEADS * jnp.arange(1, ATTN_HEADS + 1, dtype=F32))
    bias = -slopes[:, None, None] * dist.astype(F32) * LOG2E
    return jnp.where(valid[:, None], bias[None], -jnp.inf)


def _attention(p, sinks, batch, seq):
    t = p.shape[0]
    nb = seq // WINDOW

    def cur(b, n):
        return b * nb + n

    def prev(b, n):
        return b * nb + jnp.maximum(n - 1, 0)

    return pl.pallas_call(
        _attn_kernel,
        grid=(batch, nb),
        in_specs=[
            pl.BlockSpec(memory_space=pltpu.SMEM),
            pl.BlockSpec((WINDOW, ATTN_WIDTH), lambda b, n: (cur(b, n), Q_WIN0 // ATTN_WIDTH)),
            pl.BlockSpec((WINDOW, LANE), lambda b, n: (cur(b, n), (Q_WIN0 + ATTN_WIDTH) // LANE)),
            pl.BlockSpec((WINDOW, P_TN), lambda b, n: (prev(b, n), KV_WIN0 // P_TN)),
            pl.BlockSpec((WINDOW, P_TN), lambda b, n: (cur(b, n), KV_WIN0 // P_TN)),
            pl.BlockSpec((1, ATTN_HEADS, 2 * WINDOW, WINDOW), lambda b, n: (jnp.minimum(n, 1), 0, 0, 0)),
        ],
        out_specs=pl.BlockSpec((WINDOW, ATTN_WIDTH), lambda b, n: (cur(b, n), 0)),
        out_shape=jax.ShapeDtypeStruct((t, ATTN_WIDTH), BF16),
        compiler_params=_cparams(("parallel", "parallel")),
        name="swa",
    )(sinks, p, p, p, p, _alibi_bias_tables())


def _merge_kernel(y_ref, a_ref, gs_ref, ga_ref, wso_ref, wao_ref, o_ref):
    y_ssm = _dot(y_ref[...], wso_ref[...])
    y_attn = _dot(a_ref[...], wao_ref[...])
    o_ref[...] = (gs_ref[...] * y_ssm + ga_ref[...] * y_attn).astype(o_ref.dtype)


def _merge(y, a, p, wso, wao, tm, tn):
    t = y.shape[0]
    gs_blk0 = P_SGATE // tn
    ga_blk0 = (P_SGATE + D_MODEL) // tn
    return pl.pallas_call(
        _merge_kernel,
        grid=(D_MODEL // tn, t // tm),
        in_specs=[
            pl.BlockSpec((tm, D_INNER), lambda j, i: (i, 0)),
            pl.BlockSpec((tm, ATTN_WIDTH), lambda j, i: (i, 0)),
            pl.BlockSpec((tm, tn), lambda j, i: (i, gs_blk0 + j)),
            pl.BlockSpec((tm, tn), lambda j, i: (i, ga_blk0 + j)),
            pl.BlockSpec((D_INNER, tn), lambda j, i: (0, j)),
            pl.BlockSpec((ATTN_WIDTH, tn), lambda j, i: (0, j)),
        ],
        out_specs=pl.BlockSpec((tm, tn), lambda j, i: (i, j)),
        out_shape=jax.ShapeDtypeStruct((t, D_MODEL), BF16),
        compiler_params=_cparams(("parallel", "parallel")),
        name="merge",
    )(y, a, p, p, wso, wao)


def _oproj_kernel(m_ref, w_ref, x_ref, o_ref):
    o_ref[...] = x_ref[...] + _dot(m_ref[...], w_ref[...])


def _oproj(merged, wo, x2, tm):
    t = x2.shape[0]
    return pl.pallas_call(
        _oproj_kernel,
        grid=(t // tm,),
        in_specs=[
            pl.BlockSpec((tm, D_MODEL), lambda i: (i, 0)),
            pl.BlockSpec((D_MODEL, D_MODEL), lambda i: (0, 0)),
            pl.BlockSpec((tm, D_MODEL), lambda i: (i, 0)),
        ],
        out_specs=pl.BlockSpec((tm, D_MODEL), lambda i: (i, 0)),
        out_shape=jax.ShapeDtypeStruct((t, D_MODEL), F32),
        compiler_params=_cparams(("parallel",)),
        name="oproj",
    )(merged, wo, x2)


def _ffn_kernel(tiles_per_seq, h_ref, halo_ref, nw_ref, wg_ref, wv_ref, cw_ref, cb_ref, wout_ref,
                fw_ref, o_ref, u_scr, acc_scr):
    i = pl.program_id(0)
    j = pl.program_id(1)
    tm = h_ref.shape[0]

    @pl.when(j == 0)
    def _():
        nw = nw_ref[...]
        first = (i % tiles_per_seq) == 0
        u_scr[0:HALO, :] = jnp.where(first, 0.0, _rms_scale(halo_ref[...], nw)).astype(BF16)
        u_scr[HALO:, :] = _rms_scale(h_ref[...], nw).astype(BF16)
        acc_scr[...] = jnp.zeros_like(acc_scr)

    gate = _dot(u_scr[...], wg_ref[...])
    val = _dot(u_scr[HALO:, :], wv_ref[...])
    conv = _causal_conv(gate, cw_ref[...], cb_ref[...], FFN_CONV, tm)
    act = 0.5 * conv * (1.0 + lax.erf(conv * (2.0 ** -0.5)))
    acc_scr[...] += _dot((act * val).astype(BF16), wout_ref[...])

    @pl.when(j == pl.num_programs(1) - 1)
    def _():
        o_ref[...] = _rms_scale(h_ref[...] + acc_scr[...], fw_ref[...])


def _ffn(h1, nw, w_in, cw, cb, wout, fw, seq, tm, tn):
    t = h1.shape[0]
    halo_blocks = tm // HALO
    val_blk0 = D_FF // tn
    return pl.pallas_call(
        functools.partial(_ffn_kernel, seq // tm),
        grid=(t // tm, D_FF // tn),
        in_specs=[
            pl.BlockSpec((tm, D_MODEL), lambda i, j: (i, 0)),
            pl.BlockSpec((HALO, D_MODEL), lambda i, j: (jnp.maximum(i * halo_blocks - 1, 0), 0)),
            pl.BlockSpec((1, D_MODEL), lambda i, j: (0, 0)),
            pl.BlockSpec((D_MODEL, tn), lambda i, j: (0, j)),
            pl.BlockSpec((D_MODEL, tn), lambda i, j: (0, val_blk0 + j)),
            pl.BlockSpec((None, FFN_CONV, tn), lambda i, j: (0, 0, j)),
            pl.BlockSpec((1, tn), lambda i, j: (0, j)),
            pl.BlockSpec((tn, D_MODEL), lambda i, j: (j, 0)),
            pl.BlockSpec((1, D_MODEL), lambda i, j: (0, 0)),
        ],
        out_specs=pl.BlockSpec((tm, D_MODEL), lambda i, j: (i, 0)),
        out_shape=jax.ShapeDtypeStruct((t, D_MODEL), F32),
        scratch_shapes=[
            pltpu.VMEM((HALO + tm, D_MODEL), BF16),
            pltpu.VMEM((tm, D_MODEL), F32),
        ],
        compiler_params=_cparams(("parallel", "arbitrary")),
        name="conv_ffn",
    )(h1, h1, nw, w_in, w_in, cw, cb, wout, fw)


def _packed_head_expand():
    e = np.zeros((2 * LANE, D_INNER), np.float32)
    for part in range(3):
        for h in range(SSM_HEADS):
            e[part * SSM_HEADS + h, h * SSM_HEAD_DIM:(h + 1) * SSM_HEAD_DIM] = 1.0
    return jnp.asarray(e, BF16)


def _pad_lanes(v, width):
    v = v.astype(F32).reshape(1, -1)
    return jnp.pad(v, ((0, 0), (0, width - v.shape[1])))


def kernel(x, attn_norm_w, w_in, ssm_conv_w, ssm_conv_b, dt_bias, a_log, d_skip, ssm_norm_w, attn_sinks, w_ssm_out, w_attn_out, w_o, ffn_norm_w, w_ffn_in, ffn_conv_w, ffn_conv_b, w_ffn_out, final_norm_w):
    batch, seq, d = x.shape
    t = batch * seq
    assert w_in.shape[0] == 1, "single-layer problem: the final norm is fused into the FFN kernel"
    tm = min(1024, seq)
    tm_half = min(512, seq)
    assert seq % tm == 0 and seq % CHUNK == 0
    x2 = x.reshape(t, d)

    u = _norm(x2, attn_norm_w[0].reshape(1, d), tm)
    w_in_t = jnp.swapaxes(w_in, 1, 2)[0]
    p = _in_proj(u, w_in_t, ssm_conv_w, ssm_conv_b, seq, tm)

    dtb = _pad_lanes(dt_bias[0], LANE)
    a_neg = _pad_lanes(-jnp.exp(a_log[0].astype(F32)), LANE)
    dsk = jnp.repeat(d_skip[0].astype(F32), SSM_HEAD_DIM).reshape(1, D_INNER)
    snw = ssm_norm_w[0].astype(F32).reshape(1, D_INNER)
    y = _ssd(p, dtb, a_neg, _packed_head_expand(), dsk, snw, batch, seq)

    attn = _attention(p, attn_sinks[0].astype(F32), batch, seq)

    merged = _merge(y, attn, p, w_ssm_out[0].astype(BF16), w_attn_out[0].astype(BF16), tm_half, 1024)
    h1 = _oproj(merged, w_o[0].astype(BF16), x2, tm_half)

    out = _ffn(h1, ffn_norm_w[0].reshape(1, d), w_ffn_in[0].astype(BF16), ffn_conv_w, ffn_conv_b,
               w_ffn_out[0].astype(BF16), final_norm_w.reshape(1, d), seq, tm_half, 512)
    return out.reshape(batch, seq, d)
```

```python
import functools
import math

import numpy as np
import jax
import jax.numpy as jnp
from jax import lax
from jax.experimental import pallas as pl
from jax.experimental.pallas import tpu as pltpu

F32 = jnp.float32
BF16 = jnp.bfloat16

D_MODEL = 2048
D_INNER = 4096
SSM_HEADS = 64
SSM_HEAD_DIM = 64
SSM_GROUPS = 8
SSM_STATE = 128
SSM_CONV = 4
CHUNK = 128
SSD_CHUNKS_PER_STEP = 2
GROUP_WIDTH = D_INNER // SSM_GROUPS
HEADS_PER_GROUP = SSM_HEADS // SSM_GROUPS
SSM_GN = SSM_GROUPS * SSM_STATE
ATTN_HEADS = 32
ATTN_KV_HEADS = 4
ATTN_GROUP = ATTN_HEADS // ATTN_KV_HEADS
ATTN_HEAD_DIM = 64
ATTN_WIDTH = ATTN_HEADS * ATTN_HEAD_DIM
KV_WIDTH = ATTN_KV_HEADS * ATTN_HEAD_DIM
WINDOW = 128
D_FF = 5632
FFN_CONV = 3
NORM_EPS = 1e-5
LOG2E = math.log2(math.e)

P_Z = 0
P_XS = D_INNER
P_B = 2 * D_INNER
P_C = P_B + SSM_GN
P_DT = P_C + SSM_GN
P_Q = P_DT + SSM_HEADS
P_K = P_Q + ATTN_WIDTH
P_V = P_K + KV_WIDTH
P_GATE = P_V + KV_WIDTH
P_TN = 1024
P_SUB = 256
P_CONV_BLK0 = P_XS // P_TN
P_CONV_BLK1 = P_DT // P_TN
P_RAW_BLK1 = pl.cdiv(P_GATE, P_TN)
P_SGATE = P_RAW_BLK1 * P_TN
P_WIDTH = P_SGATE + 2 * D_MODEL

LANE = 128
HALO = 16
SHIFT = P_Q % LANE
VMEM_LIMIT = 56 * 1024 * 1024


def _cparams(semantics):
    return pltpu.CompilerParams(dimension_semantics=semantics, vmem_limit_bytes=VMEM_LIMIT)


def _dot(a, b):
    return jnp.dot(a, b, preferred_element_type=F32)


def _dot_nt(a, b):
    return lax.dot_general(a, b, (((1,), (1,)), ((), ())), preferred_element_type=F32)


def _split3(v):
    hi = v.astype(BF16).astype(F32)
    r1 = v - hi
    mid = r1.astype(BF16).astype(F32)
    lo = (r1 - mid).astype(BF16).astype(F32)
    return hi, mid, lo


def _rms_scale(x, w):
    ms = jnp.mean(x * x, axis=-1, keepdims=True)
    return x * lax.rsqrt(ms + NORM_EPS) * w


def _sigmoid(v):
    return 0.5 * jnp.tanh(0.5 * v) + 0.5


def _silu(v):
    h = 0.5 * v
    return h + h * jnp.tanh(h)


def _causal_conv(ext, w, b, taps, rows):
    acc = b + w[taps - 1:taps] * ext[HALO:HALO + rows]
    for k in range(taps - 1):
        lo = HALO - (taps - 1) + k
        acc = acc + w[k:k + 1] * ext[lo:lo + rows]
    return acc


def _norm_kernel(x_ref, w_ref, o_ref):
    o_ref[...] = _rms_scale(x_ref[...], w_ref[...]).astype(o_ref.dtype)


def _norm(x2, w, tm):
    t, d = x2.shape
    return pl.pallas_call(
        _norm_kernel,
        grid=(t // tm,),
        in_specs=[pl.BlockSpec((tm, d), lambda i: (i, 0)), pl.BlockSpec((1, d), lambda i: (0, 0))],
        out_specs=pl.BlockSpec((tm, d), lambda i: (i, 0)),
        out_shape=jax.ShapeDtypeStruct((t, d), BF16),
        compiler_params=_cparams(("parallel",)),
        name="norm",
    )(x2, w)


def _in_proj_kernel(tiles_per_seq, n_row_tiles, u_ref, halo_ref, w_ref, cw_ref, cb_ref, o_ref,
                    w_scr, lhs_scr, raw_even, raw_odd):
    j = pl.program_id(0)
    i = pl.program_id(1)
    tm = u_ref.shape[0]
    is_conv = (j >= P_CONV_BLK0) & (j < P_CONV_BLK1)
    sub = [slice(s * P_SUB, (s + 1) * P_SUB) for s in range(P_TN // P_SUB)]

    @pl.when(i == 0)
    def _():
        w_scr[...] = w_ref[...].T.astype(BF16)

    def direct(epilogue):
        for cols in sub:
            o_ref[:, cols] = epilogue(_dot(u_ref[...], w_scr[:, cols]))

    in_rows = i < n_row_tiles

    @pl.when(in_rows & (j < P_CONV_BLK0))
    def _():
        direct(_silu)

    @pl.when(in_rows & (j >= P_CONV_BLK1) & (j < P_RAW_BLK1))
    def _():
        direct(lambda acc: acc)

    @pl.when(in_rows & (j >= P_RAW_BLK1))
    def _():
        direct(_sigmoid)

    def conv_lhs():
        first = (i % tiles_per_seq) == 0
        halo = halo_ref[...]
        lhs_scr[0:HALO, :] = jnp.where(first, jnp.zeros_like(halo), halo)
        lhs_scr[HALO:, :] = u_ref[...]

    def conv_project(raw, cols):
        raw[:, cols] = _dot(lhs_scr[...], w_scr[:, cols])

    def conv_finish(raw, cols):
        o_ref[:, cols] = _silu(_causal_conv(raw[:, cols], cw_ref[:, cols], cb_ref[:, cols], SSM_CONV, tm))

    @pl.when(is_conv & (i == 0))
    def _():
        conv_lhs()
        for cols in sub:
            conv_project(raw_even, cols)

    n_k = lhs_scr.shape[1] // P_SUB
    mid = HALO + tm // 2
    halves = ((0, mid), (mid, HALO + tm))
    n_pieces = n_k * len(halves)
    rchunk = tm // n_pieces

    for parity, (mine, other) in enumerate(((raw_even, raw_odd), (raw_odd, raw_even))):
        @pl.when(is_conv & (i > 0) & in_rows & (i % 2 == parity))
        def _(mine=mine, other=other):
            conv_lhs()
            for cols in sub:
                accs = [None] * len(halves)
                for piece in range(n_pieces):
                    kc, hi = divmod(piece, len(halves))
                    m0, m1 = halves[hi]
                    ks = slice(kc * P_SUB, (kc + 1) * P_SUB)
                    part = _dot(lhs_scr[m0:m1, ks], w_scr[ks, cols])
                    accs[hi] = part if accs[hi] is None else accs[hi] + part
                    r0 = piece * rchunk
                    o_ref[r0:r0 + rchunk, cols] = _silu(_causal_conv(
                        other[r0:r0 + HALO + rchunk, cols], cw_ref[:, cols], cb_ref[:, cols], SSM_CONV, rchunk))
                for hi, (m0, m1) in enumerate(halves):
                    mine[m0:m1, cols] = accs[hi]

    last = raw_odd if n_row_tiles % 2 == 0 else raw_even

    @pl.when(is_conv & (i == n_row_tiles))
    def _():
        for cols in sub:
            conv_finish(last, cols)


def _in_proj(u, w_in_t, cw, cb, seq, tm):
    t, d = u.shape
    n_row_tiles = t // tm
    halo_blocks = tm // HALO
    conv_blk = lambda j: jnp.clip(j - P_CONV_BLK0, 0, P_CONV_BLK1 - P_CONV_BLK0 - 1)
    is_conv = lambda j: (j >= P_CONV_BLK0) & (j < P_CONV_BLK1)
    row_tile = lambda i: jnp.minimum(i, n_row_tiles - 1)
    out_tile = lambda j, i: jnp.where(is_conv(j), jnp.maximum(i - 1, 0), row_tile(i))
    w_row = lambda j: pl.multiple_of(
        jnp.where(j < P_RAW_BLK1, j * P_TN, P_GATE + (j - P_RAW_BLK1) * P_TN), 64)
    return pl.pallas_call(
        functools.partial(_in_proj_kernel, seq // tm, n_row_tiles),
        grid=(P_WIDTH // P_TN, n_row_tiles + 1),
        in_specs=[
            pl.BlockSpec((tm, d), lambda j, i: (row_tile(i), 0)),
            pl.BlockSpec((HALO, d), lambda j, i: (jnp.maximum(row_tile(i) * halo_blocks - 1, 0), 0)),
            pl.BlockSpec((pl.Element(P_TN), pl.Element(d)), lambda j, i: (w_row(j), 0)),
            pl.BlockSpec((None, SSM_CONV, P_TN), lambda j, i: (0, 0, conv_blk(j))),
            pl.BlockSpec((1, P_TN), lambda j, i: (0, conv_blk(j))),
        ],
        out_specs=pl.BlockSpec((tm, P_TN), lambda j, i: (out_tile(j, i), j)),
        out_shape=jax.ShapeDtypeStruct((t, P_WIDTH), F32),
        scratch_shapes=[
            pltpu.VMEM((d, P_TN), BF16),
            pltpu.VMEM((HALO + tm, d), BF16),
            pltpu.VMEM((HALO + tm, P_TN), F32),
            pltpu.VMEM((HALO + tm, P_TN), F32),
        ],
        compiler_params=_cparams(("parallel", "arbitrary")),
        name="in_proj",
    )(u, u, w_in_t, cw, cb)


def _ssd_kernel(sz_ref, xs_ref, bc_ref, dt_ref, dtb_ref, a_ref, epk_ref, dsk_ref, nw_ref,
                y_ref, state_scr):
    @pl.when(pl.program_id(1) == 0)
    def _():
        state_scr[...] = jnp.zeros_like(state_scr)

    for sub in range(SSD_CHUNKS_PER_STEP):
        rows = slice(sub * CHUNK, (sub + 1) * CHUNK)
        _ssd_chunk(sz_ref.at[rows], xs_ref.at[rows], bc_ref.at[rows], dt_ref.at[rows], dtb_ref, a_ref, epk_ref,
                   dsk_ref, nw_ref, y_ref.at[rows], state_scr)


def _ssd_chunk(sz_ref, xs_ref, bc_ref, dt_ref, dtb_ref, a_ref, epk_ref, dsk_ref, nw_ref,
               y_ref, state_scr):
    v = dt_ref[...] + dtb_ref[...]
    dt = jnp.maximum(v, 0.0) + jnp.log1p(jnp.exp(-jnp.abs(v)))
    ad = dt * a_ref[...]

    row = lax.broadcasted_iota(jnp.int32, (CHUNK, CHUNK), 0)
    col = lax.broadcasted_iota(jnp.int32, (CHUNK, CHUNK), 1)
    causal = row >= col
    tril = causal.astype(BF16)
    a_cum = sum(_dot(tril, p.astype(BF16)) for p in _split3(ad)) * LOG2E
    a_cum_t = a_cum.T
    ea = jnp.exp2(a_cum)
    ds = jnp.exp2(a_cum[CHUNK - 1:CHUNK, :] - a_cum)

    low_half = col < SSM_HEADS

    def packed(q):
        hi, mid, lo = _split3(q)
        first = jnp.where(low_half, hi, pltpu.roll(mid, SSM_HEADS, axis=1))
        return jnp.concatenate([first, lo], axis=1).astype(BF16)

    stack = jnp.concatenate([packed(dt), packed(ea), packed(ds)], axis=0)

    lane_low = lax.broadcasted_iota(jnp.int32, (CHUNK, LANE), 1) < SSM_HEAD_DIM
    neg_inf = jnp.float32(-jnp.inf)
    for g in range(SSM_GROUPS):
        c0 = g * GROUP_WIDTH
        expanded = _dot(stack, epk_ref[:, c0:c0 + GROUP_WIDTH])
        dtx = expanded[0:CHUNK]
        eax = expanded[CHUNK:2 * CHUNK]
        dsx = expanded[2 * CHUNK:3 * CHUNK]
        xs = xs_ref[:, c0:c0 + GROUP_WIDTH]
        bm = bc_ref[:, g * SSM_STATE:(g + 1) * SSM_STATE]
        cm16 = bc_ref[:, SSM_GN + g * SSM_STATE:SSM_GN + (g + 1) * SSM_STATE].astype(BF16)
        xdt = xs * dtx
        cb = _dot_nt(cm16, bm.astype(BF16))

        y_parts = []
        for pr in range(HEADS_PER_GROUP // 2):
            ms = []
            for h in (g * HEADS_PER_GROUP + 2 * pr, g * HEADS_PER_GROUP + 2 * pr + 1):
                seg = a_cum[:, h:h + 1] - a_cum_t[h:h + 1, :]
                decay = jnp.exp2(jnp.where(causal, seg, neg_inf))
                ms.append((cb * decay).astype(BF16))
            m_pair = jnp.concatenate(ms, axis=1)
            xp = xdt[:, pr * LANE:(pr + 1) * LANE]
            x_pair = jnp.concatenate([jnp.where(lane_low, xp, 0.0), jnp.where(lane_low, 0.0, xp)],
                                     axis=0).astype(BF16)
            y_parts.append(_dot(m_pair, x_pair))
        y = jnp.concatenate(y_parts, axis=1)

        state = state_scr[g]
        y = y + _dot(cm16, state.astype(BF16)) * eax
        new_state = _dot(bm.T.astype(BF16), (xdt * dsx).astype(BF16))
        state_scr[g] = state * eax[CHUNK - 1:CHUNK, :] + new_state

        y = (y + dsk_ref[:, c0:c0 + GROUP_WIDTH] * xs) * sz_ref[:, c0:c0 + GROUP_WIDTH]
        y_ref[:, c0:c0 + GROUP_WIDTH] = _rms_scale(y, nw_ref[:, c0:c0 + GROUP_WIDTH]).astype(y_ref.dtype)


def _ssd(p, dtb, a_neg, epk, dsk, nw, batch, seq):
    t = p.shape[0]
    rows_per_step = SSD_CHUNKS_PER_STEP * CHUNK
    nc = seq // rows_per_step

    def rows(b, c):
        return b * nc + c

    const = lambda shape: pl.BlockSpec(shape, lambda b, c: (0, 0))
    return pl.pallas_call(
        _ssd_kernel,
        grid=(batch, nc),
        in_specs=[
            pl.BlockSpec((rows_per_step, D_INNER), lambda b, c: (rows(b, c), P_Z // D_INNER)),
            pl.BlockSpec((rows_per_step, D_INNER), lambda b, c: (rows(b, c), P_XS // D_INNER)),
            pl.BlockSpec((rows_per_step, 2 * SSM_GN), lambda b, c: (rows(b, c), P_B // (2 * SSM_GN))),
            pl.BlockSpec((rows_per_step, LANE), lambda b, c: (rows(b, c), P_DT // LANE)),
            const((1, LANE)), const((1, LANE)),
            const((2 * LANE, D_INNER)),
            const((1, D_INNER)), const((1, D_INNER)),
        ],
        out_specs=pl.BlockSpec((rows_per_step, D_INNER), lambda b, c: (rows(b, c), 0)),
        out_shape=jax.ShapeDtypeStruct((t, D_INNER), BF16),
        scratch_shapes=[pltpu.VMEM((SSM_GROUPS, SSM_STATE, GROUP_WIDTH), F32)],
        compiler_params=_cparams(("parallel", "arbitrary")),
        name="ssd",
    )(p, p, p, p, dtb, a_neg, epk, dsk, nw)


Q_WIN0 = P_Q - SHIFT
KV_WIN0 = (P_K // P_TN) * P_TN
K_LOC = P_K - KV_WIN0
V_LOC = P_V - KV_WIN0
V_SLAB0 = (V_LOC // LANE) * LANE
V_SLAB1 = V_SLAB0 + KV_WIDTH + LANE


def _attn_kernel(sink_ref, q_ref, qx_ref, kvp_ref, kvc_ref, bias_ref, o_ref):
    c = (ATTN_HEAD_DIM ** -0.5) * LOG2E
    kv = jnp.concatenate([kvp_ref[...], kvc_ref[...]], axis=0)
    v_t = kv[:, V_SLAB0:V_SLAB1].T.astype(BF16)
    q_t = jnp.concatenate([q_ref[...].T, qx_ref[...].T], axis=0).astype(BF16)
    for kh in range(ATTN_KV_HEADS):
        d0 = kh * ATTN_HEAD_DIM
        k_h = kv[:, K_LOC + d0:K_LOC + d0 + ATTN_HEAD_DIM].astype(BF16)
        heads = [kh * ATTN_GROUP + r for r in range(ATTN_GROUP)]
        q_g = jnp.concatenate(
            [q_t[SHIFT + h * ATTN_HEAD_DIM:SHIFT + (h + 1) * ATTN_HEAD_DIM, :] for h in heads],
            axis=1)
        s_t = _dot(k_h, q_g)
        ps, dens = [], []
        for r, h in enumerate(heads):
            l2 = s_t[:, r * WINDOW:(r + 1) * WINDOW] * c + bias_ref[0, h]
            sink2 = sink_ref[h] * LOG2E
            m = jnp.maximum(jnp.max(l2, axis=0, keepdims=True), sink2)
            p = jnp.exp2(l2 - m)
            dens.append(jnp.sum(p, axis=0, keepdims=True) + jnp.exp2(sink2 - m))
            ps.append(p.astype(BF16))
        p_t = jnp.concatenate(ps, axis=1)
        v0 = V_LOC - V_SLAB0 + d0
        o_t = _dot(v_t[v0:v0 + ATTN_HEAD_DIM, :], p_t)
        o_g = jnp.concatenate([o_t[:, r * WINDOW:(r + 1) * WINDOW] / dens[r]
                               for r in range(ATTN_GROUP)], axis=0)
        w0 = kh * ATTN_GROUP * ATTN_HEAD_DIM
        o_ref[:, w0:w0 + ATTN_GROUP * ATTN_HEAD_DIM] = o_g.T.astype(o_ref.dtype)


def _alibi_bias_tables():
    j = jnp.arange(2 * WINDOW)[:, None]
    l = jnp.arange(WINDOW)[None, :]
    dist = l - j + WINDOW
    band = (dist >= 0) & (dist < WINDOW)
    valid = jnp.stack([band & (j >= WINDOW), band])
    slopes = jnp.exp2(-8.0 / ATTN_HEADS * jnp.arange(1, ATTN_HEADS + 1, dtype=F32))
    bias = -slopes[:, None, None] * dist.astype(F32) * LOG2E
    return jnp.where(valid[:, None], bias[None], -jnp.inf)


def _attention(p, sinks, batch, seq):
    t = p.shape[0]
    nb = seq // WINDOW

    def cur(b, n):
        return b * nb + n

    def prev(b, n):
        return b * nb + jnp.maximum(n - 1, 0)

    return pl.pallas_call(
        _attn_kernel,
        grid=(batch, nb),
        in_specs=[
            pl.BlockSpec(memory_space=pltpu.SMEM),
            pl.BlockSpec((WINDOW, ATTN_WIDTH), lambda b, n: (cur(b, n), Q_WIN0 // ATTN_WIDTH)),
            pl.BlockSpec((WINDOW, LANE), lambda b, n: (cur(b, n), (Q_WIN0 + ATTN_WIDTH) // LANE)),
            pl.BlockSpec((WINDOW, P_TN), lambda b, n: (prev(b, n), KV_WIN0 // P_TN)),
            pl.BlockSpec((WINDOW, P_TN), lambda b, n: (cur(b, n), KV_WIN0 // P_TN)),
            pl.BlockSpec((1, ATTN_HEADS, 2 * WINDOW, WINDOW), lambda b, n: (jnp.minimum(n, 1), 0, 0, 0)),
        ],
        out_specs=pl.BlockSpec((WINDOW, ATTN_WIDTH), lambda b, n: (cur(b, n), 0)),
        out_shape=jax.ShapeDtypeStruct((t, ATTN_WIDTH), BF16),
        compiler_params=_cparams(("parallel", "parallel")),
        name="swa",
    )(sinks, p, p, p, p, _alibi_bias_tables())


def _merge_kernel(y_ref, a_ref, gs_ref, ga_ref, wso_ref, wao_ref, o_ref):
    y_ssm = _dot(y_ref[...], wso_ref[...])
    y_attn = _dot(a_ref[...], wao_ref[...])
    o_ref[...] = (gs_ref[...] * y_ssm + ga_ref[...] * y_attn).astype(o_ref.dtype)


def _merge(y, a, p, wso, wao, tm, tn):
    t = y.shape[0]
    gs_blk0 = P_SGATE // tn
    ga_blk0 = (P_SGATE + D_MODEL) // tn
    return pl.pallas_call(
        _merge_kernel,
        grid=(D_MODEL // tn, t // tm),
        in_specs=[
            pl.BlockSpec((tm, D_INNER), lambda j, i: (i, 0)),
            pl.BlockSpec((tm, ATTN_WIDTH), lambda j, i: (i, 0)),
            pl.BlockSpec((tm, tn), lambda j, i: (i, gs_blk0 + j)),
            pl.BlockSpec((tm, tn), lambda j, i: (i, ga_blk0 + j)),
            pl.BlockSpec((D_INNER, tn), lambda j, i: (0, j)),
            pl.BlockSpec((ATTN_WIDTH, tn), lambda j, i: (0, j)),
        ],
        out_specs=pl.BlockSpec((tm, tn), lambda j, i: (i, j)),
        out_shape=jax.ShapeDtypeStruct((t, D_MODEL), BF16),
        compiler_params=_cparams(("parallel", "parallel")),
        name="merge",
    )(y, a, p, p, wso, wao)


def _oproj_kernel(m_ref, w_ref, x_ref, o_ref):
    o_ref[...] = x_ref[...] + _dot(m_ref[...], w_ref[...])


def _oproj(merged, wo, x2, tm):
    t = x2.shape[0]
    return pl.pallas_call(
        _oproj_kernel,
        grid=(t // tm,),
        in_specs=[
            pl.BlockSpec((tm, D_MODEL), lambda i: (i, 0)),
            pl.BlockSpec((D_MODEL, D_MODEL), lambda i: (0, 0)),
            pl.BlockSpec((tm, D_MODEL), lambda i: (i, 0)),
        ],
        out_specs=pl.BlockSpec((tm, D_MODEL), lambda i: (i, 0)),
        out_shape=jax.ShapeDtypeStruct((t, D_MODEL), F32),
        compiler_params=_cparams(("parallel",)),
        name="oproj",
    )(merged, wo, x2)


def _ffn_kernel(tiles_per_seq, h_ref, halo_ref, nw_ref, wg_ref, wv_ref, cw_ref, cb_ref, wout_ref,
                fw_ref, o_ref, u_scr, acc_scr):
    i = pl.program_id(0)
    j = pl.program_id(1)
    tm = h_ref.shape[0]

    @pl.when(j == 0)
    def _():
        nw = nw_ref[...]
        first = (i % tiles_per_seq) == 0
        u_scr[0:HALO, :] = jnp.where(first, 0.0, _rms_scale(halo_ref[...], nw)).astype(BF16)
        u_scr[HALO:, :] = _rms_scale(h_ref[...], nw).astype(BF16)
        acc_scr[...] = jnp.zeros_like(acc_scr)

    gate = _dot(u_scr[...], wg_ref[...])
    val = _dot(u_scr[HALO:, :], wv_ref[...])
    conv = _causal_conv(gate, cw_ref[...], cb_ref[...], FFN_CONV, tm)
    act = 0.5 * conv * (1.0 + lax.erf(conv * (2.0 ** -0.5)))
    acc_scr[...] += _dot((act * val).astype(BF16), wout_ref[...])

    @pl.when(j == pl.num_programs(1) - 1)
    def _():
        o_ref[...] = _rms_scale(h_ref[...] + acc_scr[...], fw_ref[...])


def _ffn(h1, nw, w_in, cw, cb, wout, fw, seq, tm, tn):
    t = h1.shape[0]
    halo_blocks = tm // HALO
    val_blk0 = D_FF // tn
    return pl.pallas_call(
        functools.partial(_ffn_kernel, seq // tm),
        grid=(t // tm, D_FF // tn),
        in_specs=[
            pl.BlockSpec((tm, D_MODEL), lambda i, j: (i, 0)),
            pl.BlockSpec((HALO, D_MODEL), lambda i, j: (jnp.maximum(i * halo_blocks - 1, 0), 0)),
            pl.BlockSpec((1, D_MODEL), lambda i, j: (0, 0)),
            pl.BlockSpec((D_MODEL, tn), lambda i, j: (0, j)),
            pl.BlockSpec((D_MODEL, tn), lambda i, j: (0, val_blk0 + j)),
            pl.BlockSpec((None, FFN_CONV, tn), lambda i, j: (0, 0, j)),
            pl.BlockSpec((1, tn), lambda i, j: (0, j)),
            pl.BlockSpec((tn, D_MODEL), lambda i, j: (j, 0)),
            pl.BlockSpec((1, D_MODEL), lambda i, j: (0, 0)),
        ],
        out_specs=pl.BlockSpec((tm, D_MODEL), lambda i, j: (i, 0)),
        out_shape=jax.ShapeDtypeStruct((t, D_MODEL), F32),
        scratch_shapes=[
            pltpu.VMEM((HALO + tm, D_MODEL), BF16),
            pltpu.VMEM((tm, D_MODEL), F32),
        ],
        compiler_params=_cparams(("parallel", "arbitrary")),
        name="conv_ffn",
    )(h1, h1, nw, w_in, w_in, cw, cb, wout, fw)


def _packed_head_expand():
    e = np.zeros((2 * LANE, D_INNER), np.float32)
    for part in range(3):
        for h in range(SSM_HEADS):
            e[part * SSM_HEADS + h, h * SSM_HEAD_DIM:(h + 1) * SSM_HEAD_DIM] = 1.0
    return jnp.asarray(e, BF16)


def _pad_lanes(v, width):
    v = v.astype(F32).reshape(1, -1)
    return jnp.pad(v, ((0, 0), (0, width - v.shape[1])))


def kernel(x, attn_norm_w, w_in, ssm_conv_w, ssm_conv_b, dt_bias, a_log, d_skip, ssm_norm_w, attn_sinks, w_ssm_out, w_attn_out, w_o, ffn_norm_w, w_ffn_in, ffn_conv_w, ffn_conv_b, w_ffn_out, final_norm_w):
    batch, seq, d = x.shape
    t = batch * seq
    assert w_in.shape[0] == 1, "single-layer problem: the final norm is fused into the FFN kernel"
    tm = min(1024, seq)
    tm_half = min(512, seq)
    assert seq % tm == 0 and seq % (SSD_CHUNKS_PER_STEP * CHUNK) == 0
    x2 = x.reshape(t, d)

    u = _norm(x2, attn_norm_w[0].reshape(1, d), tm)
    w_in_t = jnp.swapaxes(w_in, 1, 2)[0]
    p = _in_proj(u, w_in_t, ssm_conv_w, ssm_conv_b, seq, tm)

    dtb = _pad_lanes(dt_bias[0], LANE)
    a_neg = _pad_lanes(-jnp.exp(a_log[0].astype(F32)), LANE)
    dsk = jnp.repeat(d_skip[0].astype(F32), SSM_HEAD_DIM).reshape(1, D_INNER)
    snw = ssm_norm_w[0].astype(F32).reshape(1, D_INNER)
    y = _ssd(p, dtb, a_neg, _packed_head_expand(), dsk, snw, batch, seq)

    attn = _attention(p, attn_sinks[0].astype(F32), batch, seq)

    merged = _merge(y, attn, p, w_ssm_out[0].astype(BF16), w_attn_out[0].astype(BF16), tm_half, 1024)
    h1 = _oproj(merged, w_o[0].astype(BF16), x2, tm_half)

    out = _ffn(h1, ffn_norm_w[0].reshape(1, d), w_ffn_in[0].astype(BF16), ffn_conv_w, ffn_conv_b,
               w_ffn_out[0].astype(BF16), final_norm_w.reshape(1, d), seq, tm_half, 512)
    return out.reshape(batch, seq, d)
```

```python
import functools
import math

import numpy as np
import jax
import jax.numpy as jnp
from jax import lax
from jax.experimental import pallas as pl
from jax.experimental.pallas import tpu as pltpu

F32 = jnp.float32
BF16 = jnp.bfloat16

D_MODEL = 2048
D_INNER = 4096
SSM_HEADS = 64
SSM_HEAD_DIM = 64
SSM_GROUPS = 8
SSM_STATE = 128
SSM_CONV = 4
CHUNK = 128
SSD_CHUNKS_PER_STEP = 2
GROUP_WIDTH = D_INNER // SSM_GROUPS
HEADS_PER_GROUP = SSM_HEADS // SSM_GROUPS
SSM_GN = SSM_GROUPS * SSM_STATE
ATTN_HEADS = 32
ATTN_KV_HEADS = 4
ATTN_GROUP = ATTN_HEADS // ATTN_KV_HEADS
ATTN_HEAD_DIM = 64
ATTN_WIDTH = ATTN_HEADS * ATTN_HEAD_DIM
KV_WIDTH = ATTN_KV_HEADS * ATTN_HEAD_DIM
WINDOW = 128
D_FF = 5632
FFN_CONV = 3
NORM_EPS = 1e-5
LOG2E = math.log2(math.e)

P_Z = 0
P_XS = D_INNER
P_B = 2 * D_INNER
P_C = P_B + SSM_GN
P_DT = P_C + SSM_GN
P_Q = P_DT + SSM_HEADS
P_K = P_Q + ATTN_WIDTH
P_V = P_K + KV_WIDTH
P_GATE = P_V + KV_WIDTH
P_TN = 1024
P_SUB = 256
P_CONV_BLK0 = P_XS // P_TN
P_CONV_BLK1 = P_DT // P_TN
P_RAW_BLK1 = pl.cdiv(P_GATE, P_TN)
P_SGATE = P_RAW_BLK1 * P_TN
P_WIDTH = P_SGATE + 2 * D_MODEL

LANE = 128
HALO = 16
SHIFT = P_Q % LANE
VMEM_LIMIT = 56 * 1024 * 1024


def _cparams(semantics):
    return pltpu.CompilerParams(dimension_semantics=semantics, vmem_limit_bytes=VMEM_LIMIT)


def _dot(a, b):
    return jnp.dot(a, b, preferred_element_type=F32)


def _dot_nt(a, b):
    return lax.dot_general(a, b, (((1,), (1,)), ((), ())), preferred_element_type=F32)


def _split3(v):
    hi = v.astype(BF16).astype(F32)
    r1 = v - hi
    mid = r1.astype(BF16).astype(F32)
    lo = (r1 - mid).astype(BF16).astype(F32)
    return hi, mid, lo


def _rms_scale(x, w):
    ms = jnp.mean(x * x, axis=-1, keepdims=True)
    return x * lax.rsqrt(ms + NORM_EPS) * w


def _sigmoid(v):
    return 0.5 * jnp.tanh(0.5 * v) + 0.5


def _silu(v):
    h = 0.5 * v
    return h + h * jnp.tanh(h)


def _causal_conv(ext, w, b, taps, rows):
    acc = b + w[taps - 1:taps] * ext[HALO:HALO + rows]
    for k in range(taps - 1):
        lo = HALO - (taps - 1) + k
        acc = acc + w[k:k + 1] * ext[lo:lo + rows]
    return acc


def _norm_kernel(x_ref, w_ref, o_ref):
    o_ref[...] = _rms_scale(x_ref[...], w_ref[...]).astype(o_ref.dtype)


def _norm(x2, w, tm):
    t, d = x2.shape
    return pl.pallas_call(
        _norm_kernel,
        grid=(t // tm,),
        in_specs=[pl.BlockSpec((tm, d), lambda i: (i, 0)), pl.BlockSpec((1, d), lambda i: (0, 0))],
        out_specs=pl.BlockSpec((tm, d), lambda i: (i, 0)),
        out_shape=jax.ShapeDtypeStruct((t, d), BF16),
        compiler_params=_cparams(("parallel",)),
        name="norm",
    )(x2, w)


def _in_proj_kernel(tiles_per_seq, u_ref, halo_ref, w_ref, cw_ref, cb_ref, o_ref, w_scr, lhs_scr):
    j = pl.program_id(0)
    i = pl.program_id(1)
    tm = u_ref.shape[0]
    sub = [slice(s * P_SUB, (s + 1) * P_SUB) for s in range(P_TN // P_SUB)]

    @pl.when(i == 0)
    def _():
        w_scr[...] = w_ref[...].T.astype(BF16)

    def direct(epilogue):
        for cols in sub:
            o_ref[:, cols] = epilogue(_dot(u_ref[...], w_scr[:, cols]))

    @pl.when(j < P_CONV_BLK0)
    def _():
        direct(_silu)

    @pl.when((j >= P_CONV_BLK1) & (j < P_RAW_BLK1))
    def _():
        direct(lambda acc: acc)

    @pl.when(j >= P_RAW_BLK1)
    def _():
        direct(_sigmoid)

    @pl.when((j >= P_CONV_BLK0) & (j < P_CONV_BLK1))
    def _():
        first = (i % tiles_per_seq) == 0
        halo = halo_ref[...]
        lhs_scr[0:HALO, :] = jnp.where(first, jnp.zeros_like(halo), halo)
        lhs_scr[HALO:, :] = u_ref[...]
        for cols in sub:
            ext = _dot(lhs_scr[...], w_scr[:, cols])
            o_ref[:, cols] = _silu(_causal_conv(ext, cw_ref[:, cols], cb_ref[:, cols], SSM_CONV, tm))


def _in_proj(u, w_in_t, cw, cb, seq, tm):
    t, d = u.shape
    halo_blocks = tm // HALO
    conv_blk = lambda j: jnp.clip(j - P_CONV_BLK0, 0, P_CONV_BLK1 - P_CONV_BLK0 - 1)
    w_row = lambda j: pl.multiple_of(
        jnp.where(j < P_RAW_BLK1, j * P_TN, P_GATE + (j - P_RAW_BLK1) * P_TN), 64)
    return pl.pallas_call(
        functools.partial(_in_proj_kernel, seq // tm),
        grid=(P_WIDTH // P_TN, t // tm),
        in_specs=[
            pl.BlockSpec((tm, d), lambda j, i: (i, 0)),
            pl.BlockSpec((HALO, d), lambda j, i: (jnp.maximum(i * halo_blocks - 1, 0), 0)),
            pl.BlockSpec((pl.Element(P_TN), pl.Element(d)), lambda j, i: (w_row(j), 0)),
            pl.BlockSpec((None, SSM_CONV, P_TN), lambda j, i: (0, 0, conv_blk(j))),
            pl.BlockSpec((1, P_TN), lambda j, i: (0, conv_blk(j))),
        ],
        out_specs=pl.BlockSpec((tm, P_TN), lambda j, i: (i, j)),
        out_shape=jax.ShapeDtypeStruct((t, P_WIDTH), F32),
        scratch_shapes=[pltpu.VMEM((d, P_TN), BF16), pltpu.VMEM((HALO + tm, d), BF16)],
        compiler_params=_cparams(("parallel", "arbitrary")),
        name="in_proj",
    )(u, u, w_in_t, cw, cb)


def _ssd_kernel(sz_ref, xs_ref, bc_ref, dt_ref, dtb_ref, a_ref, epk_ref, dsk_ref, nw_ref,
                y_ref, state_scr):
    @pl.when(pl.program_id(1) == 0)
    def _():
        state_scr[...] = jnp.zeros_like(state_scr)

    for sub in range(SSD_CHUNKS_PER_STEP):
        rows = slice(sub * CHUNK, (sub + 1) * CHUNK)
        _ssd_chunk(sz_ref.at[rows], xs_ref.at[rows], bc_ref.at[rows], dt_ref.at[rows], dtb_ref, a_ref, epk_ref,
                   dsk_ref, nw_ref, y_ref.at[rows], state_scr)


def _ssd_chunk(sz_ref, xs_ref, bc_ref, dt_ref, dtb_ref, a_ref, epk_ref, dsk_ref, nw_ref,
               y_ref, state_scr):
    v = dt_ref[...] + dtb_ref[...]
    dt = jnp.maximum(v, 0.0) + jnp.log1p(jnp.exp(-jnp.abs(v)))
    ad = dt * a_ref[...]

    row = lax.broadcasted_iota(jnp.int32, (CHUNK, CHUNK), 0)
    col = lax.broadcasted_iota(jnp.int32, (CHUNK, CHUNK), 1)
    causal = row >= col
    tril = causal.astype(BF16)
    a_cum = sum(_dot(tril, p.astype(BF16)) for p in _split3(ad)) * LOG2E
    a_cum_t = a_cum.T
    ea = jnp.exp2(a_cum)
    ds = jnp.exp2(a_cum[CHUNK - 1:CHUNK, :] - a_cum)

    low_half = col < SSM_HEADS

    def packed(q):
        hi, mid, lo = _split3(q)
        first = jnp.where(low_half, hi, pltpu.roll(mid, SSM_HEADS, axis=1))
        return jnp.concatenate([first, lo], axis=1).astype(BF16)

    stack = jnp.concatenate([packed(dt), packed(ea), packed(ds)], axis=0)

    lane_low = lax.broadcasted_iota(jnp.int32, (CHUNK, LANE), 1) < SSM_HEAD_DIM
    neg_inf = jnp.float32(-jnp.inf)
    for g in range(SSM_GROUPS):
        c0 = g * GROUP_WIDTH
        expanded = _dot(stack, epk_ref[:, c0:c0 + GROUP_WIDTH])
        dtx = expanded[0:CHUNK]
        eax = expanded[CHUNK:2 * CHUNK]
        dsx = expanded[2 * CHUNK:3 * CHUNK]
        xs = xs_ref[:, c0:c0 + GROUP_WIDTH]
        bm = bc_ref[:, g * SSM_STATE:(g + 1) * SSM_STATE]
        cm16 = bc_ref[:, SSM_GN + g * SSM_STATE:SSM_GN + (g + 1) * SSM_STATE].astype(BF16)
        xdt = xs * dtx
        cb = _dot_nt(cm16, bm.astype(BF16))

        y_parts = []
        for pr in range(HEADS_PER_GROUP // 2):
            ms = []
            for h in (g * HEADS_PER_GROUP + 2 * pr, g * HEADS_PER_GROUP + 2 * pr + 1):
                seg = a_cum[:, h:h + 1] - a_cum_t[h:h + 1, :]
                decay = jnp.exp2(jnp.where(causal, seg, neg_inf))
                ms.append((cb * decay).astype(BF16))
            m_pair = jnp.concatenate(ms, axis=1)
            xp = xdt[:, pr * LANE:(pr + 1) * LANE]
            x_pair = jnp.concatenate([jnp.where(lane_low, xp, 0.0), jnp.where(lane_low, 0.0, xp)],
                                     axis=0).astype(BF16)
            y_parts.append(_dot(m_pair, x_pair))
        y = jnp.concatenate(y_parts, axis=1)

        state = state_scr[g]
        y = y + _dot(cm16, state.astype(BF16)) * eax
        new_state = _dot(bm.T.astype(BF16), (xdt * dsx).astype(BF16))
        state_scr[g] = state * eax[CHUNK - 1:CHUNK, :] + new_state

        y = (y + dsk_ref[:, c0:c0 + GROUP_WIDTH] * xs) * sz_ref[:, c0:c0 + GROUP_WIDTH]
        y_ref[:, c0:c0 + GROUP_WIDTH] = _rms_scale(y, nw_ref[:, c0:c0 + GROUP_WIDTH]).astype(y_ref.dtype)


def _ssd(p, dtb, a_neg, epk, dsk, nw, batch, seq):
    t = p.shape[0]
    rows_per_step = SSD_CHUNKS_PER_STEP * CHUNK
    nc = seq // rows_per_step

    def rows(b, c):
        return b * nc + c

    const = lambda shape: pl.BlockSpec(shape, lambda b, c: (0, 0))
    return pl.pallas_call(
        _ssd_kernel,
        grid=(batch, nc),
        in_specs=[
            pl.BlockSpec((rows_per_step, D_INNER), lambda b, c: (rows(b, c), P_Z // D_INNER)),
            pl.BlockSpec((rows_per_step, D_INNER), lambda b, c: (rows(b, c), P_XS // D_INNER)),
            pl.BlockSpec((rows_per_step, 2 * SSM_GN), lambda b, c: (rows(b, c), P_B // (2 * SSM_GN))),
            pl.BlockSpec((rows_per_step, LANE), lambda b, c: (rows(b, c), P_DT // LANE)),
            const((1, LANE)), const((1, LANE)),
            const((2 * LANE, D_INNER)),
            const((1, D_INNER)), const((1, D_INNER)),
        ],
        out_specs=pl.BlockSpec((rows_per_step, D_INNER), lambda b, c: (rows(b, c), 0)),
        out_shape=jax.ShapeDtypeStruct((t, D_INNER), BF16),
        scratch_shapes=[pltpu.VMEM((SSM_GROUPS, SSM_STATE, GROUP_WIDTH), F32)],
        compiler_params=_cparams(("parallel", "arbitrary")),
        name="ssd",
    )(p, p, p, p, dtb, a_neg, epk, dsk, nw)


Q_WIN0 = P_Q - SHIFT
KV_WIN0 = (P_K // P_TN) * P_TN
K_LOC = P_K - KV_WIN0
V_LOC = P_V - KV_WIN0
V_SLAB0 = (V_LOC // LANE) * LANE
V_SLAB1 = V_SLAB0 + KV_WIDTH + LANE


def _attn_kernel(sink_ref, q_ref, qx_ref, kvp_ref, kvc_ref, bias_ref, o_ref):
    c = (ATTN_HEAD_DIM ** -0.5) * LOG2E
    key_j = lax.broadcasted_iota(jnp.int32, (WINDOW, WINDOW), 0)
    query_l = lax.broadcasted_iota(jnp.int32, (WINDOW, WINDOW), 1)
    from_prev = key_j > query_l
    kv = jnp.concatenate([kvp_ref[...], kvc_ref[...]], axis=0)
    v_t = kv[:, V_SLAB0:V_SLAB1].T.astype(BF16)
    q_t = jnp.concatenate([q_ref[...].T, qx_ref[...].T], axis=0).astype(BF16)
    for kh in range(ATTN_KV_HEADS):
        d0 = kh * ATTN_HEAD_DIM
        k_h = kv[:, K_LOC + d0:K_LOC + d0 + ATTN_HEAD_DIM].astype(BF16)
        heads = [kh * ATTN_GROUP + r for r in range(ATTN_GROUP)]
        q_g = jnp.concatenate(
            [q_t[SHIFT + h * ATTN_HEAD_DIM:SHIFT + (h + 1) * ATTN_HEAD_DIM, :] for h in heads],
            axis=1)
        s_t = _dot(k_h, q_g)
        prev_ps, cur_ps, dens = [], [], []
        for r, h in enumerate(heads):
            s_h = s_t[:, r * WINDOW:(r + 1) * WINDOW]
            l2 = jnp.where(from_prev, s_h[:WINDOW], s_h[WINDOW:]) * c + bias_ref[0, h]
            sink2 = sink_ref[h] * LOG2E
            m = jnp.maximum(jnp.max(l2, axis=0, keepdims=True), sink2)
            p = jnp.exp2(l2 - m)
            dens.append(jnp.sum(p, axis=0, keepdims=True) + jnp.exp2(sink2 - m))
            prev_ps.append(jnp.where(from_prev, p, 0.0).astype(BF16))
            cur_ps.append(jnp.where(from_prev, 0.0, p).astype(BF16))
        p_t = jnp.concatenate([jnp.concatenate(prev_ps, axis=1), jnp.concatenate(cur_ps, axis=1)],
                              axis=0)
        v0 = V_LOC - V_SLAB0 + d0
        o_t = _dot(v_t[v0:v0 + ATTN_HEAD_DIM, :], p_t)
        o_g = jnp.concatenate([o_t[:, r * WINDOW:(r + 1) * WINDOW] / dens[r]
                               for r in range(ATTN_GROUP)], axis=0)
        w0 = kh * ATTN_GROUP * ATTN_HEAD_DIM
        o_ref[:, w0:w0 + ATTN_GROUP * ATTN_HEAD_DIM] = o_g.T.astype(o_ref.dtype)


def _alibi_bias_tables():
    j = jnp.arange(WINDOW)[:, None]
    l = jnp.arange(WINDOW)[None, :]
    from_prev = j > l
    dist = jnp.where(from_prev, l - j + WINDOW, l - j).astype(F32)
    slopes = jnp.exp2(-8.0 / ATTN_HEADS * jnp.arange(1, ATTN_HEADS + 1, dtype=F32))
    bias = -slopes[:, None, None] * dist * LOG2E
    first = jnp.where(from_prev[None], -jnp.inf, bias)
    return jnp.stack([first, bias])


def _attention(p, sinks, batch, seq):
    t = p.shape[0]
    nb = seq // WINDOW

    def cur(b, n):
        return b * nb + n

    def prev(b, n):
        return b * nb + jnp.maximum(n - 1, 0)

    return pl.pallas_call(
        _attn_kernel,
        grid=(batch, nb),
        in_specs=[
            pl.BlockSpec(memory_space=pltpu.SMEM),
            pl.BlockSpec((WINDOW, ATTN_WIDTH), lambda b, n: (cur(b, n), Q_WIN0 // ATTN_WIDTH)),
            pl.BlockSpec((WINDOW, LANE), lambda b, n: (cur(b, n), (Q_WIN0 + ATTN_WIDTH) // LANE)),
            pl.BlockSpec((WINDOW, P_TN), lambda b, n: (prev(b, n), KV_WIN0 // P_TN)),
            pl.BlockSpec((WINDOW, P_TN), lambda b, n: (cur(b, n), KV_WIN0 // P_TN)),
            pl.BlockSpec((1, ATTN_HEADS, WINDOW, WINDOW), lambda b, n: (jnp.minimum(n, 1), 0, 0, 0)),
        ],
        out_specs=pl.BlockSpec((WINDOW, ATTN_WIDTH), lambda b, n: (cur(b, n), 0)),
        out_shape=jax.ShapeDtypeStruct((t, ATTN_WIDTH), BF16),
        compiler_params=_cparams(("parallel", "parallel")),
        name="swa",
    )(sinks, p, p, p, p, _alibi_bias_tables())


def _merge_kernel(y_ref, a_ref, gs_ref, ga_ref, wso_ref, wao_ref, o_ref):
    y_ssm = _dot(y_ref[...], wso_ref[...])
    y_attn = _dot(a_ref[...], wao_ref[...])
    o_ref[...] = (gs_ref[...] * y_ssm + ga_ref[...] * y_attn).astype(o_ref.dtype)


def _merge(y, a, p, wso, wao, tm, tn):
    t = y.shape[0]
    gs_blk0 = P_SGATE // tn
    ga_blk0 = (P_SGATE + D_MODEL) // tn
    return pl.pallas_call(
        _merge_kernel,
        grid=(D_MODEL // tn, t // tm),
        in_specs=[
            pl.BlockSpec((tm, D_INNER), lambda j, i: (i, 0)),
            pl.BlockSpec((tm, ATTN_WIDTH), lambda j, i: (i, 0)),
            pl.BlockSpec((tm, tn), lambda j, i: (i, gs_blk0 + j)),
            pl.BlockSpec((tm, tn), lambda j, i: (i, ga_blk0 + j)),
            pl.BlockSpec((D_INNER, tn), lambda j, i: (0, j)),
            pl.BlockSpec((ATTN_WIDTH, tn), lambda j, i: (0, j)),
        ],
        out_specs=pl.BlockSpec((tm, tn), lambda j, i: (i, j)),
        out_shape=jax.ShapeDtypeStruct((t, D_MODEL), BF16),
        compiler_params=_cparams(("parallel", "parallel")),
        name="merge",
    )(y, a, p, p, wso, wao)


def _oproj_kernel(m_ref, w_ref, x_ref, o_ref):
    o_ref[...] = x_ref[...] + _dot(m_ref[...], w_ref[...])


def _oproj(merged, wo, x2, tm):
    t = x2.shape[0]
    return pl.pallas_call(
        _oproj_kernel,
        grid=(t // tm,),
        in_specs=[
            pl.BlockSpec((tm, D_MODEL), lambda i: (i, 0)),
            pl.BlockSpec((D_MODEL, D_MODEL), lambda i: (0, 0)),
            pl.BlockSpec((tm, D_MODEL), lambda i: (i, 0)),
        ],
        out_specs=pl.BlockSpec((tm, D_MODEL), lambda i: (i, 0)),
        out_shape=jax.ShapeDtypeStruct((t, D_MODEL), F32),
        compiler_params=_cparams(("parallel",)),
        name="oproj",
    )(merged, wo, x2)


def _ffn_kernel(tiles_per_seq, h_ref, halo_ref, nw_ref, wg_ref, wv_ref, cw_ref, cb_ref, wout_ref,
                fw_ref, o_ref, u_scr, acc_scr):
    i = pl.program_id(0)
    j = pl.program_id(1)
    tm = h_ref.shape[0]

    @pl.when(j == 0)
    def _():
        nw = nw_ref[...]
        first = (i % tiles_per_seq) == 0
        u_scr[0:HALO, :] = jnp.where(first, 0.0, _rms_scale(halo_ref[...], nw)).astype(BF16)
        u_scr[HALO:, :] = _rms_scale(h_ref[...], nw).astype(BF16)
        acc_scr[...] = jnp.zeros_like(acc_scr)

    gate = _dot(u_scr[...], wg_ref[...])
    val = _dot(u_scr[HALO:, :], wv_ref[...])
    conv = _causal_conv(gate, cw_ref[...], cb_ref[...], FFN_CONV, tm)
    act = 0.5 * conv * (1.0 + lax.erf(conv * (2.0 ** -0.5)))
    acc_scr[...] += _dot((act * val).astype(BF16), wout_ref[...])

    @pl.when(j == pl.num_programs(1) - 1)
    def _():
        o_ref[...] = _rms_scale(h_ref[...] + acc_scr[...], fw_ref[...])


def _ffn(h1, nw, w_in, cw, cb, wout, fw, seq, tm, tn):
    t = h1.shape[0]
    halo_blocks = tm // HALO
    val_blk0 = D_FF // tn
    return pl.pallas_call(
        functools.partial(_ffn_kernel, seq // tm),
        grid=(t // tm, D_FF // tn),
        in_specs=[
            pl.BlockSpec((tm, D_MODEL), lambda i, j: (i, 0)),
            pl.BlockSpec((HALO, D_MODEL), lambda i, j: (jnp.maximum(i * halo_blocks - 1, 0), 0)),
            pl.BlockSpec((1, D_MODEL), lambda i, j: (0, 0)),
            pl.BlockSpec((D_MODEL, tn), lambda i, j: (0, j)),
            pl.BlockSpec((D_MODEL, tn), lambda i, j: (0, val_blk0 + j)),
            pl.BlockSpec((None, FFN_CONV, tn), lambda i, j: (0, 0, j)),
            pl.BlockSpec((1, tn), lambda i, j: (0, j)),
            pl.BlockSpec((tn, D_MODEL), lambda i, j: (j, 0)),
            pl.BlockSpec((1, D_MODEL), lambda i, j: (0, 0)),
        ],
        out_specs=pl.BlockSpec((tm, D_MODEL), lambda i, j: (i, 0)),
        out_shape=jax.ShapeDtypeStruct((t, D_MODEL), F32),
        scratch_shapes=[
            pltpu.VMEM((HALO + tm, D_MODEL), BF16),
            pltpu.VMEM((tm, D_MODEL), F32),
        ],
        compiler_params=_cparams(("parallel", "arbitrary")),
        name="conv_ffn",
    )(h1, h1, nw, w_in, w_in, cw, cb, wout, fw)


def _packed_head_expand():
    e = np.zeros((2 * LANE, D_INNER), np.float32)
    for part in range(3):
        for h in range(SSM_HEADS):
            e[part * SSM_HEADS + h, h * SSM_HEAD_DIM:(h + 1) * SSM_HEAD_DIM] = 1.0
    return jnp.asarray(e, BF16)


def _pad_lanes(v, width):
    v = v.astype(F32).reshape(1, -1)
    return jnp.pad(v, ((0, 0), (0, width - v.shape[1])))


def kernel(x, attn_norm_w, w_in, ssm_conv_w, ssm_conv_b, dt_bias, a_log, d_skip, ssm_norm_w, attn_sinks, w_ssm_out, w_attn_out, w_o, ffn_norm_w, w_ffn_in, ffn_conv_w, ffn_conv_b, w_ffn_out, final_norm_w):
    batch, seq, d = x.shape
    t = batch * seq
    assert w_in.shape[0] == 1, "single-layer problem: the final norm is fused into the FFN kernel"
    tm = min(1024, seq)
    tm_half = min(512, seq)
    assert seq % tm == 0 and seq % (SSD_CHUNKS_PER_STEP * CHUNK) == 0
    x2 = x.reshape(t, d)

    u = _norm(x2, attn_norm_w[0].reshape(1, d), tm)
    w_in_t = jnp.swapaxes(w_in, 1, 2)[0]
    p = _in_proj(u, w_in_t, ssm_conv_w, ssm_conv_b, seq, tm)

    dtb = _pad_lanes(dt_bias[0], LANE)
    a_neg = _pad_lanes(-jnp.exp(a_log[0].astype(F32)), LANE)
    dsk = jnp.repeat(d_skip[0].astype(F32), SSM_HEAD_DIM).reshape(1, D_INNER)
    snw = ssm_norm_w[0].astype(F32).reshape(1, D_INNER)
    y = _ssd(p, dtb, a_neg, _packed_head_expand(), dsk, snw, batch, seq)

    attn = _attention(p, attn_sinks[0].astype(F32), batch, seq)

    merged = _merge(y, attn, p, w_ssm_out[0].astype(BF16), w_attn_out[0].astype(BF16), tm_half, 1024)
    h1 = _oproj(merged, w_o[0].astype(BF16), x2, tm_half)

    out = _ffn(h1, ffn_norm_w[0].reshape(1, d), w_ffn_in[0].astype(BF16), ffn_conv_w, ffn_conv_b,
               w_ffn_out[0].astype(BF16), final_norm_w.reshape(1, d), seq, tm_half, 512)
    return out.reshape(batch, seq, d)
```

```python
import functools
import math

import numpy as np
import jax
import jax.numpy as jnp
from jax import lax
from jax.experimental import pallas as pl
from jax.experimental.pallas import tpu as pltpu

F32 = jnp.float32
BF16 = jnp.bfloat16

D_MODEL = 2048
D_INNER = 4096
SSM_HEADS = 64
SSM_HEAD_DIM = 64
SSM_GROUPS = 8
SSM_STATE = 128
SSM_CONV = 4
CHUNK = 128
SSD_CHUNKS_PER_STEP = 2
GROUP_WIDTH = D_INNER // SSM_GROUPS
HEADS_PER_GROUP = SSM_HEADS // SSM_GROUPS
SSM_GN = SSM_GROUPS * SSM_STATE
ATTN_HEADS = 32
ATTN_KV_HEADS = 4
ATTN_GROUP = ATTN_HEADS // ATTN_KV_HEADS
ATTN_HEAD_DIM = 64
ATTN_WIDTH = ATTN_HEADS * ATTN_HEAD_DIM
KV_WIDTH = ATTN_KV_HEADS * ATTN_HEAD_DIM
WINDOW = 128
D_FF = 5632
FFN_CONV = 3
NORM_EPS = 1e-5
LOG2E = math.log2(math.e)

P_Z = 0
P_XS = D_INNER
P_B = 2 * D_INNER
P_C = P_B + SSM_GN
P_DT = P_C + SSM_GN
P_Q = P_DT + SSM_HEADS
P_K = P_Q + ATTN_WIDTH
P_V = P_K + KV_WIDTH
P_GATE = P_V + KV_WIDTH
P_TN = 1024
P_SUB = 256
P_CONV_BLK0 = P_XS // P_TN
P_CONV_BLK1 = P_DT // P_TN
P_RAW_BLK1 = pl.cdiv(P_GATE, P_TN)
P_SGATE = P_RAW_BLK1 * P_TN
P_WIDTH = P_SGATE + 2 * D_MODEL

LANE = 128
HALO = 16
SHIFT = P_Q % LANE
VMEM_LIMIT = 56 * 1024 * 1024


def _cparams(semantics):
    return pltpu.CompilerParams(dimension_semantics=semantics, vmem_limit_bytes=VMEM_LIMIT)


def _dot(a, b):
    return jnp.dot(a, b, preferred_element_type=F32)


def _dot_nt(a, b):
    return lax.dot_general(a, b, (((1,), (1,)), ((), ())), preferred_element_type=F32)


def _split3(v):
    hi = v.astype(BF16).astype(F32)
    r1 = v - hi
    mid = r1.astype(BF16).astype(F32)
    lo = (r1 - mid).astype(BF16).astype(F32)
    return hi, mid, lo


def _rms_scale(x, w):
    ms = jnp.mean(x * x, axis=-1, keepdims=True)
    return x * lax.rsqrt(ms + NORM_EPS) * w


def _sigmoid(v):
    return 0.5 * jnp.tanh(0.5 * v) + 0.5


def _silu_of_half(h):
    return h + h * jnp.tanh(h)


def _causal_conv(ext, w, b, taps, rows):
    acc = b + w[taps - 1:taps] * ext[HALO:HALO + rows]
    for k in range(taps - 1):
        lo = HALO - (taps - 1) + k
        acc = acc + w[k:k + 1] * ext[lo:lo + rows]
    return acc


def _norm_kernel(x_ref, w_ref, o_ref):
    o_ref[...] = _rms_scale(x_ref[...], w_ref[...]).astype(o_ref.dtype)


def _norm(x2, w, tm):
    t, d = x2.shape
    return pl.pallas_call(
        _norm_kernel,
        grid=(t // tm,),
        in_specs=[pl.BlockSpec((tm, d), lambda i: (i, 0)), pl.BlockSpec((1, d), lambda i: (0, 0))],
        out_specs=pl.BlockSpec((tm, d), lambda i: (i, 0)),
        out_shape=jax.ShapeDtypeStruct((t, d), BF16),
        compiler_params=_cparams(("parallel",)),
        name="norm",
    )(x2, w)


def _in_proj_kernel(tiles_per_seq, u_ref, halo_ref, w_ref, cw_ref, cb_ref, o_ref, w_scr, lhs_scr):
    j = pl.program_id(0)
    i = pl.program_id(1)
    tm = u_ref.shape[0]
    sub = [slice(s * P_SUB, (s + 1) * P_SUB) for s in range(P_TN // P_SUB)]

    @pl.when(i == 0)
    def _():
        scale = jnp.where(j < P_CONV_BLK1, 0.5, 1.0)
        w_scr[...] = (w_ref[...] * scale).T.astype(BF16)

    def direct(epilogue, tiles=sub):
        for cols in tiles:
            o_ref[:, cols] = epilogue(_dot(u_ref[...], w_scr[:, cols]))

    @pl.when(j < P_CONV_BLK0)
    def _():
        direct(_silu_of_half)

    @pl.when((j >= P_CONV_BLK1) & (j < P_RAW_BLK1 - 1))
    def _():
        direct(lambda acc: acc)

    @pl.when(j == P_RAW_BLK1 - 1)
    def _():
        used = pl.cdiv(P_GATE - (P_RAW_BLK1 - 1) * P_TN, P_SUB)
        direct(lambda acc: acc, sub[:used])
        for cols in sub[used:]:
            o_ref[:, cols] = jnp.zeros((tm, P_SUB), F32)

    @pl.when(j >= P_RAW_BLK1)
    def _():
        direct(_sigmoid)

    @pl.when((j >= P_CONV_BLK0) & (j < P_CONV_BLK1))
    def _():
        first = (i % tiles_per_seq) == 0
        halo = halo_ref[...]
        lhs_scr[0:HALO, :] = jnp.where(first, jnp.zeros_like(halo), halo)
        lhs_scr[HALO:, :] = u_ref[...]
        for cols in sub:
            half_ext = _dot(lhs_scr[...], w_scr[:, cols])
            half_conv = _causal_conv(half_ext, cw_ref[:, cols], 0.5 * cb_ref[:, cols], SSM_CONV, tm)
            o_ref[:, cols] = _silu_of_half(half_conv)


def _in_proj(u, w_in_t, cw, cb, seq, tm):
    t, d = u.shape
    halo_blocks = tm // HALO
    conv_blk = lambda j: jnp.clip(j - P_CONV_BLK0, 0, P_CONV_BLK1 - P_CONV_BLK0 - 1)
    w_row = lambda j: pl.multiple_of(
        jnp.where(j < P_RAW_BLK1, j * P_TN, P_GATE + (j - P_RAW_BLK1) * P_TN), 64)
    return pl.pallas_call(
        functools.partial(_in_proj_kernel, seq // tm),
        grid=(P_WIDTH // P_TN, t // tm),
        in_specs=[
            pl.BlockSpec((tm, d), lambda j, i: (i, 0)),
            pl.BlockSpec((HALO, d), lambda j, i: (jnp.maximum(i * halo_blocks - 1, 0), 0)),
            pl.BlockSpec((pl.Element(P_TN), pl.Element(d)), lambda j, i: (w_row(j), 0)),
            pl.BlockSpec((None, SSM_CONV, P_TN), lambda j, i: (0, 0, conv_blk(j))),
            pl.BlockSpec((1, P_TN), lambda j, i: (0, conv_blk(j))),
        ],
        out_specs=pl.BlockSpec((tm, P_TN), lambda j, i: (i, j)),
        out_shape=jax.ShapeDtypeStruct((t, P_WIDTH), F32),
        scratch_shapes=[pltpu.VMEM((d, P_TN), BF16), pltpu.VMEM((HALO + tm, d), BF16)],
        compiler_params=_cparams(("parallel", "arbitrary")),
        name="in_proj",
    )(u, u, w_in_t, cw, cb)


def _ssd_kernel(sz_ref, xs_ref, bc_ref, dt_ref, dtb_ref, a_ref, epk_ref, dsk_ref, nw_ref,
                y_ref, state_scr):
    @pl.when(pl.program_id(1) == 0)
    def _():
        state_scr[...] = jnp.zeros_like(state_scr)

    for sub in range(SSD_CHUNKS_PER_STEP):
        rows = slice(sub * CHUNK, (sub + 1) * CHUNK)
        _ssd_chunk(sz_ref.at[rows], xs_ref.at[rows], bc_ref.at[rows], dt_ref.at[rows], dtb_ref, a_ref, epk_ref,
                   dsk_ref, nw_ref, y_ref.at[rows], state_scr)


def _ssd_chunk(sz_ref, xs_ref, bc_ref, dt_ref, dtb_ref, a_ref, epk_ref, dsk_ref, nw_ref,
               y_ref, state_scr):
    v = dt_ref[...] + dtb_ref[...]
    dt = jnp.maximum(v, 0.0) + jnp.log1p(jnp.exp(-jnp.abs(v)))
    ad = dt * a_ref[...]

    row = lax.broadcasted_iota(jnp.int32, (CHUNK, CHUNK), 0)
    col = lax.broadcasted_iota(jnp.int32, (CHUNK, CHUNK), 1)
    causal = row >= col
    tril = causal.astype(BF16)
    a_cum = sum(_dot(tril, p.astype(BF16)) for p in _split3(ad)) * LOG2E
    a_cum_t = a_cum.T
    ea = jnp.exp2(a_cum)
    ds = jnp.exp2(a_cum[CHUNK - 1:CHUNK, :] - a_cum)

    low_half = col < SSM_HEADS

    def packed(q):
        hi, mid, lo = _split3(q)
        first = jnp.where(low_half, hi, pltpu.roll(mid, SSM_HEADS, axis=1))
        return jnp.concatenate([first, lo], axis=1).astype(BF16)

    stack = jnp.concatenate([packed(dt), packed(ea), packed(ds)], axis=0)

    lane_low = lax.broadcasted_iota(jnp.int32, (CHUNK, LANE), 1) < SSM_HEAD_DIM
    neg_inf = jnp.float32(-jnp.inf)
    for g in range(SSM_GROUPS):
        c0 = g * GROUP_WIDTH
        expanded = _dot(stack, epk_ref[:, c0:c0 + GROUP_WIDTH])
        dtx = expanded[0:CHUNK]
        eax = expanded[CHUNK:2 * CHUNK]
        dsx = expanded[2 * CHUNK:3 * CHUNK]
        xs = xs_ref[:, c0:c0 + GROUP_WIDTH]
        bm = bc_ref[:, g * SSM_STATE:(g + 1) * SSM_STATE]
        cm16 = bc_ref[:, SSM_GN + g * SSM_STATE:SSM_GN + (g + 1) * SSM_STATE].astype(BF16)
        xdt = xs * dtx
        cb = _dot_nt(cm16, bm.astype(BF16))

        y_parts = []
        for pr in range(HEADS_PER_GROUP // 2):
            ms = []
            for h in (g * HEADS_PER_GROUP + 2 * pr, g * HEADS_PER_GROUP + 2 * pr + 1):
                seg = a_cum[:, h:h + 1] - a_cum_t[h:h + 1, :]
                decay = jnp.exp2(jnp.where(causal, seg, neg_inf))
                ms.append((cb * decay).astype(BF16))
            m_pair = jnp.concatenate(ms, axis=1)
            xp = xdt[:, pr * LANE:(pr + 1) * LANE]
            x_pair = jnp.concatenate([jnp.where(lane_low, xp, 0.0), jnp.where(lane_low, 0.0, xp)],
                                     axis=0).astype(BF16)
            y_parts.append(_dot(m_pair, x_pair))
        y = jnp.concatenate(y_parts, axis=1)

        state = state_scr[g]
        y = y + _dot(cm16, state.astype(BF16)) * eax
        new_state = _dot(bm.T.astype(BF16), (xdt * dsx).astype(BF16))
        state_scr[g] = state * eax[CHUNK - 1:CHUNK, :] + new_state

        y = (y + dsk_ref[:, c0:c0 + GROUP_WIDTH] * xs) * sz_ref[:, c0:c0 + GROUP_WIDTH]
        y_ref[:, c0:c0 + GROUP_WIDTH] = _rms_scale(y, nw_ref[:, c0:c0 + GROUP_WIDTH]).astype(y_ref.dtype)


def _ssd(p, dtb, a_neg, epk, dsk, nw, batch, seq):
    t = p.shape[0]
    rows_per_step = SSD_CHUNKS_PER_STEP * CHUNK
    nc = seq // rows_per_step

    def rows(b, c):
        return b * nc + c

    const = lambda shape: pl.BlockSpec(shape, lambda b, c: (0, 0))
    return pl.pallas_call(
        _ssd_kernel,
        grid=(batch, nc),
        in_specs=[
            pl.BlockSpec((rows_per_step, D_INNER), lambda b, c: (rows(b, c), P_Z // D_INNER)),
            pl.BlockSpec((rows_per_step, D_INNER), lambda b, c: (rows(b, c), P_XS // D_INNER)),
            pl.BlockSpec((rows_per_step, 2 * SSM_GN), lambda b, c: (rows(b, c), P_B // (2 * SSM_GN))),
            pl.BlockSpec((rows_per_step, LANE), lambda b, c: (rows(b, c), P_DT // LANE)),
            const((1, LANE)), const((1, LANE)),
            const((2 * LANE, D_INNER)),
            const((1, D_INNER)), const((1, D_INNER)),
        ],
        out_specs=pl.BlockSpec((rows_per_step, D_INNER), lambda b, c: (rows(b, c), 0)),
        out_shape=jax.ShapeDtypeStruct((t, D_INNER), BF16),
        scratch_shapes=[pltpu.VMEM((SSM_GROUPS, SSM_STATE, GROUP_WIDTH), F32)],
        compiler_params=_cparams(("parallel", "arbitrary")),
        name="ssd",
    )(p, p, p, p, dtb, a_neg, epk, dsk, nw)


Q_WIN0 = P_Q - SHIFT
KV_WIN0 = (P_K // P_TN) * P_TN
K_LOC = P_K - KV_WIN0
V_LOC = P_V - KV_WIN0
V_SLAB0 = (V_LOC // LANE) * LANE
V_SLAB1 = V_SLAB0 + KV_WIDTH + LANE


def _attn_kernel(sink_ref, q_ref, qx_ref, kvp_ref, kvc_ref, bias_ref, o_ref):
    c = (ATTN_HEAD_DIM ** -0.5) * LOG2E
    key_j = lax.broadcasted_iota(jnp.int32, (WINDOW, WINDOW), 0)
    query_l = lax.broadcasted_iota(jnp.int32, (WINDOW, WINDOW), 1)
    from_prev = key_j > query_l
    kv = jnp.concatenate([kvp_ref[...], kvc_ref[...]], axis=0)
    v_t = kv[:, V_SLAB0:V_SLAB1].T.astype(BF16)
    q_t = jnp.concatenate([q_ref[...].T, qx_ref[...].T], axis=0).astype(BF16)
    for kh in range(ATTN_KV_HEADS):
        d0 = kh * ATTN_HEAD_DIM
        k_h = kv[:, K_LOC + d0:K_LOC + d0 + ATTN_HEAD_DIM].astype(BF16)
        heads = [kh * ATTN_GROUP + r for r in range(ATTN_GROUP)]
        q_g = jnp.concatenate(
            [q_t[SHIFT + h * ATTN_HEAD_DIM:SHIFT + (h + 1) * ATTN_HEAD_DIM, :] for h in heads],
            axis=1)
        s_t = _dot(k_h, q_g)
        prev_ps, cur_ps, dens = [], [], []
        for r, h in enumerate(heads):
            s_h = s_t[:, r * WINDOW:(r + 1) * WINDOW]
            l2 = jnp.where(from_prev, s_h[:WINDOW], s_h[WINDOW:]) * c + bias_ref[0, h]
            sink2 = sink_ref[h] * LOG2E
            m = jnp.maximum(jnp.max(l2, axis=0, keepdims=True), sink2)
            p = jnp.exp2(l2 - m)
            dens.append(jnp.sum(p, axis=0, keepdims=True) + jnp.exp2(sink2 - m))
            prev_ps.append(jnp.where(from_prev, p, 0.0).astype(BF16))
            cur_ps.append(jnp.where(from_prev, 0.0, p).astype(BF16))
        p_t = jnp.concatenate([jnp.concatenate(prev_ps, axis=1), jnp.concatenate(cur_ps, axis=1)],
                              axis=0)
        v0 = V_LOC - V_SLAB0 + d0
        o_t = _dot(v_t[v0:v0 + ATTN_HEAD_DIM, :], p_t)
        o_g = jnp.concatenate([o_t[:, r * WINDOW:(r + 1) * WINDOW] / dens[r]
                               for r in range(ATTN_GROUP)], axis=0)
        w0 = kh * ATTN_GROUP * ATTN_HEAD_DIM
        o_ref[:, w0:w0 + ATTN_GROUP * ATTN_HEAD_DIM] = o_g.T.astype(o_ref.dtype)


def _alibi_bias_tables():
    j = jnp.arange(WINDOW)[:, None]
    l = jnp.arange(WINDOW)[None, :]
    from_prev = j > l
    dist = jnp.where(from_prev, l - j + WINDOW, l - j).astype(F32)
    slopes = jnp.exp2(-8.0 / ATTN_HEADS * jnp.arange(1, ATTN_HEADS + 1, dtype=F32))
    bias = -slopes[:, None, None] * dist * LOG2E
    first = jnp.where(from_prev[None], -jnp.inf, bias)
    return jnp.stack([first, bias])


def _attention(p, sinks, batch, seq):
    t = p.shape[0]
    nb = seq // WINDOW

    def cur(b, n):
        return b * nb + n

    def prev(b, n):
        return b * nb + jnp.maximum(n - 1, 0)

    return pl.pallas_call(
        _attn_kernel,
        grid=(batch, nb),
        in_specs=[
            pl.BlockSpec(memory_space=pltpu.SMEM),
            pl.BlockSpec((WINDOW, ATTN_WIDTH), lambda b, n: (cur(b, n), Q_WIN0 // ATTN_WIDTH)),
            pl.BlockSpec((WINDOW, LANE), lambda b, n: (cur(b, n), (Q_WIN0 + ATTN_WIDTH) // LANE)),
            pl.BlockSpec((WINDOW, P_TN), lambda b, n: (prev(b, n), KV_WIN0 // P_TN)),
            pl.BlockSpec((WINDOW, P_TN), lambda b, n: (cur(b, n), KV_WIN0 // P_TN)),
            pl.BlockSpec((1, ATTN_HEADS, WINDOW, WINDOW), lambda b, n: (jnp.minimum(n, 1), 0, 0, 0)),
        ],
        out_specs=pl.BlockSpec((WINDOW, ATTN_WIDTH), lambda b, n: (cur(b, n), 0)),
        out_shape=jax.ShapeDtypeStruct((t, ATTN_WIDTH), BF16),
        compiler_params=_cparams(("parallel", "parallel")),
        name="swa",
    )(sinks, p, p, p, p, _alibi_bias_tables())


def _merge_kernel(y_ref, a_ref, gs_ref, ga_ref, wso_ref, wao_ref, o_ref):
    y_ssm = _dot(y_ref[...], wso_ref[...])
    y_attn = _dot(a_ref[...], wao_ref[...])
    o_ref[...] = (gs_ref[...] * y_ssm + ga_ref[...] * y_attn).astype(o_ref.dtype)


def _merge(y, a, p, wso, wao, tm, tn):
    t = y.shape[0]
    gs_blk0 = P_SGATE // tn
    ga_blk0 = (P_SGATE + D_MODEL) // tn
    return pl.pallas_call(
        _merge_kernel,
        grid=(D_MODEL // tn, t // tm),
        in_specs=[
            pl.BlockSpec((tm, D_INNER), lambda j, i: (i, 0)),
            pl.BlockSpec((tm, ATTN_WIDTH), lambda j, i: (i, 0)),
            pl.BlockSpec((tm, tn), lambda j, i: (i, gs_blk0 + j)),
            pl.BlockSpec((tm, tn), lambda j, i: (i, ga_blk0 + j)),
            pl.BlockSpec((D_INNER, tn), lambda j, i: (0, j)),
            pl.BlockSpec((ATTN_WIDTH, tn), lambda j, i: (0, j)),
        ],
        out_specs=pl.BlockSpec((tm, tn), lambda j, i: (i, j)),
        out_shape=jax.ShapeDtypeStruct((t, D_MODEL), BF16),
        compiler_params=_cparams(("parallel", "parallel")),
        name="merge",
    )(y, a, p, p, wso, wao)


def _oproj_kernel(m_ref, w_ref, x_ref, o_ref):
    o_ref[...] = x_ref[...] + _dot(m_ref[...], w_ref[...])


def _oproj(merged, wo, x2, tm):
    t = x2.shape[0]
    return pl.pallas_call(
        _oproj_kernel,
        grid=(t // tm,),
        in_specs=[
            pl.BlockSpec((tm, D_MODEL), lambda i: (i, 0)),
            pl.BlockSpec((D_MODEL, D_MODEL), lambda i: (0, 0)),
            pl.BlockSpec((tm, D_MODEL), lambda i: (i, 0)),
        ],
        out_specs=pl.BlockSpec((tm, D_MODEL), lambda i: (i, 0)),
        out_shape=jax.ShapeDtypeStruct((t, D_MODEL), F32),
        compiler_params=_cparams(("parallel",)),
        name="oproj",
    )(merged, wo, x2)


def _ffn_kernel(tiles_per_seq, h_ref, halo_ref, nw_ref, wg_ref, wv_ref, cw_ref, cb_ref, wout_ref,
                fw_ref, o_ref, u_scr, acc_scr):
    i = pl.program_id(0)
    j = pl.program_id(1)
    tm = h_ref.shape[0]

    @pl.when(j == 0)
    def _():
        nw = nw_ref[...]
        first = (i % tiles_per_seq) == 0
        u_scr[0:HALO, :] = jnp.where(first, 0.0, _rms_scale(halo_ref[...], nw)).astype(BF16)
        u_scr[HALO:, :] = _rms_scale(h_ref[...], nw).astype(BF16)
        acc_scr[...] = jnp.zeros_like(acc_scr)

    gate = _dot(u_scr[...], wg_ref[...])
    val = _dot(u_scr[HALO:, :], wv_ref[...])
    conv = _causal_conv(gate, cw_ref[...], cb_ref[...], FFN_CONV, tm)
    act = 0.5 * conv * (1.0 + lax.erf(conv * (2.0 ** -0.5)))
    acc_scr[...] += _dot((act * val).astype(BF16), wout_ref[...])

    @pl.when(j == pl.num_programs(1) - 1)
    def _():
        o_ref[...] = _rms_scale(h_ref[...] + acc_scr[...], fw_ref[...])


def _ffn(h1, nw, w_in, cw, cb, wout, fw, seq, tm, tn):
    t = h1.shape[0]
    halo_blocks = tm // HALO
    val_blk0 = D_FF // tn
    return pl.pallas_call(
        functools.partial(_ffn_kernel, seq // tm),
        grid=(t // tm, D_FF // tn),
        in_specs=[
            pl.BlockSpec((tm, D_MODEL), lambda i, j: (i, 0)),
            pl.BlockSpec((HALO, D_MODEL), lambda i, j: (jnp.maximum(i * halo_blocks - 1, 0), 0)),
            pl.BlockSpec((1, D_MODEL), lambda i, j: (0, 0)),
            pl.BlockSpec((D_MODEL, tn), lambda i, j: (0, j)),
            pl.BlockSpec((D_MODEL, tn), lambda i, j: (0, val_blk0 + j)),
            pl.BlockSpec((None, FFN_CONV, tn), lambda i, j: (0, 0, j)),
            pl.BlockSpec((1, tn), lambda i, j: (0, j)),
            pl.BlockSpec((tn, D_MODEL), lambda i, j: (j, 0)),
            pl.BlockSpec((1, D_MODEL), lambda i, j: (0, 0)),
        ],
        out_specs=pl.BlockSpec((tm, D_MODEL), lambda i, j: (i, 0)),
        out_shape=jax.ShapeDtypeStruct((t, D_MODEL), F32),
        scratch_shapes=[
            pltpu.VMEM((HALO + tm, D_MODEL), BF16),
            pltpu.VMEM((tm, D_MODEL), F32),
        ],
        compiler_params=_cparams(("parallel", "arbitrary")),
        name="conv_ffn",
    )(h1, h1, nw, w_in, w_in, cw, cb, wout, fw)


def _packed_head_expand():
    e = np.zeros((2 * LANE, D_INNER), np.float32)
    for part in range(3):
        for h in range(SSM_HEADS):
            e[part * SSM_HEADS + h, h * SSM_HEAD_DIM:(h + 1) * SSM_HEAD_DIM] = 1.0
    return jnp.asarray(e, BF16)


def _pad_lanes(v, width):
    v = v.astype(F32).reshape(1, -1)
    return jnp.pad(v, ((0, 0), (0, width - v.shape[1])))


def kernel(x, attn_norm_w, w_in, ssm_conv_w, ssm_conv_b, dt_bias, a_log, d_skip, ssm_norm_w, attn_sinks, w_ssm_out, w_attn_out, w_o, ffn_norm_w, w_ffn_in, ffn_conv_w, ffn_conv_b, w_ffn_out, final_norm_w):
    batch, seq, d = x.shape
    t = batch * seq
    assert w_in.shape[0] == 1, "single-layer problem: the final norm is fused into the FFN kernel"
    tm = min(1024, seq)
    tm_half = min(512, seq)
    assert seq % tm == 0 and seq % (SSD_CHUNKS_PER_STEP * CHUNK) == 0
    x2 = x.reshape(t, d)

    u = _norm(x2, attn_norm_w[0].reshape(1, d), tm)
    w_in_t = jnp.swapaxes(w_in, 1, 2)[0]
    p = _in_proj(u, w_in_t, ssm_conv_w, ssm_conv_b, seq, tm)

    dtb = _pad_lanes(dt_bias[0], LANE)
    a_neg = _pad_lanes(-jnp.exp(a_log[0].astype(F32)), LANE)
    dsk = jnp.repeat(d_skip[0].astype(F32), SSM_HEAD_DIM).reshape(1, D_INNER)
    snw = ssm_norm_w[0].astype(F32).reshape(1, D_INNER)
    y = _ssd(p, dtb, a_neg, _packed_head_expand(), dsk, snw, batch, seq)

    attn = _attention(p, attn_sinks[0].astype(F32), batch, seq)

    merged = _merge(y, attn, p, w_ssm_out[0].astype(BF16), w_attn_out[0].astype(BF16), tm_half, 1024)
    h1 = _oproj(merged, w_o[0].astype(BF16), x2, tm_half)

    out = _ffn(h1, ffn_norm_w[0].reshape(1, d), w_ffn_in[0].astype(BF16), ffn_conv_w, ffn_conv_b,
               w_ffn_out[0].astype(BF16), final_norm_w.reshape(1, d), seq, tm_half, 512)
    return out.reshape(batch, seq, d)
```

```python
import functools
import math

import numpy as np
import jax
import jax.numpy as jnp
from jax import lax
from jax.experimental import pallas as pl
from jax.experimental.pallas import tpu as pltpu

F32 = jnp.float32
BF16 = jnp.bfloat16

D_MODEL = 2048
D_INNER = 4096
SSM_HEADS = 64
SSM_HEAD_DIM = 64
SSM_GROUPS = 8
SSM_STATE = 128
SSM_CONV = 4
CHUNK = 128
SSD_CHUNKS_PER_STEP = 2
GROUP_WIDTH = D_INNER // SSM_GROUPS
HEADS_PER_GROUP = SSM_HEADS // SSM_GROUPS
SSM_GN = SSM_GROUPS * SSM_STATE
ATTN_HEADS = 32
ATTN_KV_HEADS = 4
ATTN_GROUP = ATTN_HEADS // ATTN_KV_HEADS
ATTN_HEAD_DIM = 64
ATTN_WIDTH = ATTN_HEADS * ATTN_HEAD_DIM
KV_WIDTH = ATTN_KV_HEADS * ATTN_HEAD_DIM
WINDOW = 128
D_FF = 5632
FFN_CONV = 3
NORM_EPS = 1e-5
LOG2E = math.log2(math.e)

P_Z = 0
P_XS = D_INNER
P_B = 2 * D_INNER
P_C = P_B + SSM_GN
P_DT = P_C + SSM_GN
P_Q = P_DT + SSM_HEADS
P_K = P_Q + ATTN_WIDTH
P_V = P_K + KV_WIDTH
P_GATE = P_V + KV_WIDTH
P_TN = 1024
P_SUB = 256
P_CONV_BLK0 = P_XS // P_TN
P_CONV_BLK1 = P_DT // P_TN
P_RAW_BLK1 = pl.cdiv(P_GATE, P_TN)
P_SGATE = P_RAW_BLK1 * P_TN
P_WIDTH = P_SGATE + 2 * D_MODEL

LANE = 128
HALO = 16
SHIFT = P_Q % LANE
VMEM_LIMIT = 56 * 1024 * 1024


def _cparams(semantics):
    return pltpu.CompilerParams(dimension_semantics=semantics, vmem_limit_bytes=VMEM_LIMIT)


def _dot(a, b):
    return jnp.dot(a, b, preferred_element_type=F32)


def _dot_nt(a, b):
    return lax.dot_general(a, b, (((1,), (1,)), ((), ())), preferred_element_type=F32)


def _split3(v):
    hi = v.astype(BF16).astype(F32)
    r1 = v - hi
    mid = r1.astype(BF16).astype(F32)
    lo = (r1 - mid).astype(BF16).astype(F32)
    return hi, mid, lo


def _rms_scale(x, w):
    ms = jnp.mean(x * x, axis=-1, keepdims=True)
    return x * lax.rsqrt(ms + NORM_EPS) * w


def _sigmoid(v):
    return 0.5 * jnp.tanh(0.5 * v) + 0.5


def _silu_of_half(h):
    return h + h * jnp.tanh(h)


def _causal_conv(ext, w, b, taps, rows):
    acc = b + w[taps - 1:taps] * ext[HALO:HALO + rows]
    for k in range(taps - 1):
        lo = HALO - (taps - 1) + k
        acc = acc + w[k:k + 1] * ext[lo:lo + rows]
    return acc


def _norm_kernel(x_ref, w_ref, o_ref):
    o_ref[...] = _rms_scale(x_ref[...], w_ref[...]).astype(o_ref.dtype)


def _norm(x2, w, tm):
    t, d = x2.shape
    return pl.pallas_call(
        _norm_kernel,
        grid=(t // tm,),
        in_specs=[pl.BlockSpec((tm, d), lambda i: (i, 0)), pl.BlockSpec((1, d), lambda i: (0, 0))],
        out_specs=pl.BlockSpec((tm, d), lambda i: (i, 0)),
        out_shape=jax.ShapeDtypeStruct((t, d), BF16),
        compiler_params=_cparams(("parallel",)),
        name="norm",
    )(x2, w)


def _in_proj_kernel(tiles_per_seq, u_ref, halo_ref, w_ref, cw_ref, cb_ref, o_ref, w_scr, lhs_scr):
    j = pl.program_id(0)
    i = pl.program_id(1)
    tm = u_ref.shape[0]
    sub = [slice(s * P_SUB, (s + 1) * P_SUB) for s in range(P_TN // P_SUB)]

    @pl.when(i == 0)
    def _():
        scale = jnp.where(j < P_CONV_BLK1, 0.5, 1.0)
        w_scr[...] = (w_ref[...] * scale).T.astype(BF16)

    def direct(epilogue, tiles=sub):
        for cols in tiles:
            o_ref[:, cols] = epilogue(_dot(u_ref[...], w_scr[:, cols]))

    @pl.when(j < P_CONV_BLK0)
    def _():
        direct(_silu_of_half)

    @pl.when((j >= P_CONV_BLK1) & (j < P_RAW_BLK1 - 1))
    def _():
        direct(lambda acc: acc)

    @pl.when(j == P_RAW_BLK1 - 1)
    def _():
        used = pl.cdiv(P_GATE - (P_RAW_BLK1 - 1) * P_TN, P_SUB)
        direct(lambda acc: acc, sub[:used])
        for cols in sub[used:]:
            o_ref[:, cols] = jnp.zeros((tm, P_SUB), F32)

    @pl.when(j >= P_RAW_BLK1)
    def _():
        direct(_sigmoid)

    @pl.when((j >= P_CONV_BLK0) & (j < P_CONV_BLK1))
    def _():
        first = (i % tiles_per_seq) == 0
        halo = halo_ref[...]
        lhs_scr[0:HALO, :] = jnp.where(first, jnp.zeros_like(halo), halo)
        lhs_scr[HALO:, :] = u_ref[...]
        for cols in sub:
            half_ext = _dot(lhs_scr[...], w_scr[:, cols])
            half_conv = _causal_conv(half_ext, cw_ref[:, cols], 0.5 * cb_ref[:, cols], SSM_CONV, tm)
            o_ref[:, cols] = _silu_of_half(half_conv)


def _in_proj(u, w_in_t, cw, cb, seq, tm):
    t, d = u.shape
    halo_blocks = tm // HALO
    conv_blk = lambda j: jnp.clip(j - P_CONV_BLK0, 0, P_CONV_BLK1 - P_CONV_BLK0 - 1)
    w_row = lambda j: pl.multiple_of(
        jnp.where(j < P_RAW_BLK1, j * P_TN, P_GATE + (j - P_RAW_BLK1) * P_TN), math.gcd(P_TN, P_GATE))
    return pl.pallas_call(
        functools.partial(_in_proj_kernel, seq // tm),
        grid=(P_WIDTH // P_TN, t // tm),
        in_specs=[
            pl.BlockSpec((tm, d), lambda j, i: (i, 0)),
            pl.BlockSpec((HALO, d), lambda j, i: (jnp.maximum(i * halo_blocks - 1, 0), 0)),
            pl.BlockSpec((pl.Element(P_TN), pl.Element(d)), lambda j, i: (w_row(j), 0)),
            pl.BlockSpec((None, SSM_CONV, P_TN), lambda j, i: (0, 0, conv_blk(j))),
            pl.BlockSpec((1, P_TN), lambda j, i: (0, conv_blk(j))),
        ],
        out_specs=pl.BlockSpec((tm, P_TN), lambda j, i: (i, j)),
        out_shape=jax.ShapeDtypeStruct((t, P_WIDTH), F32),
        scratch_shapes=[pltpu.VMEM((d, P_TN), BF16), pltpu.VMEM((HALO + tm, d), BF16)],
        compiler_params=_cparams(("parallel", "arbitrary")),
        name="in_proj",
    )(u, u, w_in_t, cw, cb)


def _ssd_kernel(sz_ref, xs_ref, bc_ref, dt_ref, dtb_ref, a_ref, epk_ref, dsk_ref, nw_ref,
                y_ref, state_scr):
    @pl.when(pl.program_id(1) == 0)
    def _():
        state_scr[...] = jnp.zeros_like(state_scr)

    for sub in range(SSD_CHUNKS_PER_STEP):
        rows = slice(sub * CHUNK, (sub + 1) * CHUNK)
        _ssd_chunk(sz_ref.at[rows], xs_ref.at[rows], bc_ref.at[rows], dt_ref.at[rows], dtb_ref, a_ref, epk_ref,
                   dsk_ref, nw_ref, y_ref.at[rows], state_scr)


def _ssd_chunk(sz_ref, xs_ref, bc_ref, dt_ref, dtb_ref, a_ref, epk_ref, dsk_ref, nw_ref,
               y_ref, state_scr):
    v = dt_ref[...] + dtb_ref[...]
    dt = jnp.maximum(v, 0.0) + jnp.log1p(jnp.exp(-jnp.abs(v)))
    ad = dt * a_ref[...]

    row = lax.broadcasted_iota(jnp.int32, (CHUNK, CHUNK), 0)
    col = lax.broadcasted_iota(jnp.int32, (CHUNK, CHUNK), 1)
    causal = row >= col
    tril = causal.astype(BF16)
    a_cum = sum(_dot(tril, p.astype(BF16)) for p in _split3(ad)) * LOG2E
    a_cum_t = a_cum.T

    low_half = col < SSM_HEADS

    def packed(q):
        hi, mid, lo = _split3(q)
        first = jnp.where(low_half, hi, pltpu.roll(mid, SSM_HEADS, axis=1))
        return jnp.concatenate([first, lo], axis=1).astype(BF16)

    stack = jnp.concatenate([packed(dt), packed(a_cum)], axis=0)

    lane_low = lax.broadcasted_iota(jnp.int32, (CHUNK, LANE), 1) < SSM_HEAD_DIM
    neg_inf = jnp.float32(-jnp.inf)
    for g in range(SSM_GROUPS):
        c0 = g * GROUP_WIDTH
        expanded = _dot(stack, epk_ref[:, c0:c0 + GROUP_WIDTH])
        dtx = expanded[0:CHUNK]
        acx = expanded[CHUNK:2 * CHUNK]
        eax = jnp.exp2(acx)
        dsx = jnp.exp2(acx[CHUNK - 1:CHUNK, :] - acx)
        xs = xs_ref[:, c0:c0 + GROUP_WIDTH]
        bm = bc_ref[:, g * SSM_STATE:(g + 1) * SSM_STATE]
        cm16 = bc_ref[:, SSM_GN + g * SSM_STATE:SSM_GN + (g + 1) * SSM_STATE].astype(BF16)
        xdt = xs * dtx
        cb = _dot_nt(cm16, bm.astype(BF16))

        y_parts = []
        for pr in range(HEADS_PER_GROUP // 2):
            ms = []
            for h in (g * HEADS_PER_GROUP + 2 * pr, g * HEADS_PER_GROUP + 2 * pr + 1):
                seg = a_cum[:, h:h + 1] - a_cum_t[h:h + 1, :]
                decay = jnp.exp2(jnp.where(causal, seg, neg_inf))
                ms.append((cb * decay).astype(BF16))
            m_pair = jnp.concatenate(ms, axis=1)
            xp = xdt[:, pr * LANE:(pr + 1) * LANE]
            x_pair = jnp.concatenate([jnp.where(lane_low, xp, 0.0), jnp.where(lane_low, 0.0, xp)],
                                     axis=0).astype(BF16)
            y_parts.append(_dot(m_pair, x_pair))
        y = jnp.concatenate(y_parts, axis=1)

        state = state_scr[g]
        y = y + _dot(cm16, state.astype(BF16)) * eax
        new_state = _dot(bm.T.astype(BF16), (xdt * dsx).astype(BF16))
        state_scr[g] = state * eax[CHUNK - 1:CHUNK, :] + new_state

        y = (y + dsk_ref[:, c0:c0 + GROUP_WIDTH] * xs) * sz_ref[:, c0:c0 + GROUP_WIDTH]
        y_ref[:, c0:c0 + GROUP_WIDTH] = _rms_scale(y, nw_ref[:, c0:c0 + GROUP_WIDTH]).astype(y_ref.dtype)


def _ssd(p, dtb, a_neg, epk, dsk, nw, batch, seq):
    t = p.shape[0]
    rows_per_step = SSD_CHUNKS_PER_STEP * CHUNK
    nc = seq // rows_per_step

    def rows(b, c):
        return b * nc + c

    const = lambda shape: pl.BlockSpec(shape, lambda b, c: (0, 0))
    return pl.pallas_call(
        _ssd_kernel,
        grid=(batch, nc),
        in_specs=[
            pl.BlockSpec((rows_per_step, D_INNER), lambda b, c: (rows(b, c), P_Z // D_INNER)),
            pl.BlockSpec((rows_per_step, D_INNER), lambda b, c: (rows(b, c), P_XS // D_INNER)),
            pl.BlockSpec((rows_per_step, 2 * SSM_GN), lambda b, c: (rows(b, c), P_B // (2 * SSM_GN))),
            pl.BlockSpec((rows_per_step, LANE), lambda b, c: (rows(b, c), P_DT // LANE)),
            const((1, LANE)), const((1, LANE)),
            const((2 * LANE, D_INNER)),
            const((1, D_INNER)), const((1, D_INNER)),
        ],
        out_specs=pl.BlockSpec((rows_per_step, D_INNER), lambda b, c: (rows(b, c), 0)),
        out_shape=jax.ShapeDtypeStruct((t, D_INNER), BF16),
        scratch_shapes=[pltpu.VMEM((SSM_GROUPS, SSM_STATE, GROUP_WIDTH), F32)],
        compiler_params=_cparams(("parallel", "arbitrary")),
        name="ssd",
    )(p, p, p, p, dtb, a_neg, epk, dsk, nw)


Q_WIN0 = P_Q - SHIFT
KV_WIN0 = (P_K // P_TN) * P_TN
K_LOC = P_K - KV_WIN0
V_LOC = P_V - KV_WIN0
V_SLAB0 = (V_LOC // LANE) * LANE
V_SLAB1 = V_SLAB0 + KV_WIDTH + LANE


def _attn_kernel(sink_ref, q_ref, qx_ref, kvp_ref, kvc_ref, bias_ref, o_ref):
    c = (ATTN_HEAD_DIM ** -0.5) * LOG2E
    key_j = lax.broadcasted_iota(jnp.int32, (WINDOW, WINDOW), 0)
    query_l = lax.broadcasted_iota(jnp.int32, (WINDOW, WINDOW), 1)
    from_prev = key_j > query_l
    kv = jnp.concatenate([kvp_ref[...], kvc_ref[...]], axis=0)
    v_t = kv[:, V_SLAB0:V_SLAB1].T.astype(BF16)
    q_t = jnp.concatenate([q_ref[...].T, qx_ref[...].T], axis=0).astype(BF16)
    for kh in range(ATTN_KV_HEADS):
        d0 = kh * ATTN_HEAD_DIM
        k_h = kv[:, K_LOC + d0:K_LOC + d0 + ATTN_HEAD_DIM].astype(BF16)
        heads = [kh * ATTN_GROUP + r for r in range(ATTN_GROUP)]
        q_g = jnp.concatenate(
            [q_t[SHIFT + h * ATTN_HEAD_DIM:SHIFT + (h + 1) * ATTN_HEAD_DIM, :] for h in heads],
            axis=1)
        s_t = _dot(k_h, q_g)
        prev_ps, cur_ps, dens = [], [], []
        for r, h in enumerate(heads):
            s_h = s_t[:, r * WINDOW:(r + 1) * WINDOW]
            l2 = jnp.where(from_prev, s_h[:WINDOW], s_h[WINDOW:]) * c + bias_ref[0, h]
            sink2 = sink_ref[h] * LOG2E
            m = jnp.maximum(jnp.max(l2, axis=0, keepdims=True), sink2)
            p = jnp.exp2(l2 - m)
            dens.append(jnp.sum(p, axis=0, keepdims=True) + jnp.exp2(sink2 - m))
            prev_ps.append(jnp.where(from_prev, p, 0.0).astype(BF16))
            cur_ps.append(jnp.where(from_prev, 0.0, p).astype(BF16))
        p_t = jnp.concatenate([jnp.concatenate(prev_ps, axis=1), jnp.concatenate(cur_ps, axis=1)],
                              axis=0)
        v0 = V_LOC - V_SLAB0 + d0
        o_t = _dot(v_t[v0:v0 + ATTN_HEAD_DIM, :], p_t)
        o_g = jnp.concatenate([o_t[:, r * WINDOW:(r + 1) * WINDOW] / dens[r]
                               for r in range(ATTN_GROUP)], axis=0)
        w0 = kh * ATTN_GROUP * ATTN_HEAD_DIM
        o_ref[:, w0:w0 + ATTN_GROUP * ATTN_HEAD_DIM] = o_g.T.astype(o_ref.dtype)


def _alibi_bias_tables():
    j = jnp.arange(WINDOW)[:, None]
    l = jnp.arange(WINDOW)[None, :]
    from_prev = j > l
    dist = jnp.where(from_prev, l - j + WINDOW, l - j).astype(F32)
    slopes = jnp.exp2(-8.0 / ATTN_HEADS * jnp.arange(1, ATTN_HEADS + 1, dtype=F32))
    bias = -slopes[:, None, None] * dist * LOG2E
    first = jnp.where(from_prev[None], -jnp.inf, bias)
    return jnp.stack([first, bias])


def _attention(p, sinks, batch, seq):
    t = p.shape[0]
    nb = seq // WINDOW

    def cur(b, n):
        return b * nb + n

    def prev(b, n):
        return b * nb + jnp.maximum(n - 1, 0)

    return pl.pallas_call(
        _attn_kernel,
        grid=(batch, nb),
        in_specs=[
            pl.BlockSpec(memory_space=pltpu.SMEM),
            pl.BlockSpec((WINDOW, ATTN_WIDTH), lambda b, n: (cur(b, n), Q_WIN0 // ATTN_WIDTH)),
            pl.BlockSpec((WINDOW, LANE), lambda b, n: (cur(b, n), (Q_WIN0 + ATTN_WIDTH) // LANE)),
            pl.BlockSpec((WINDOW, P_TN), lambda b, n: (prev(b, n), KV_WIN0 // P_TN)),
            pl.BlockSpec((WINDOW, P_TN), lambda b, n: (cur(b, n), KV_WIN0 // P_TN)),
            pl.BlockSpec((1, ATTN_HEADS, WINDOW, WINDOW), lambda b, n: (jnp.minimum(n, 1), 0, 0, 0)),
        ],
        out_specs=pl.BlockSpec((WINDOW, ATTN_WIDTH), lambda b, n: (cur(b, n), 0)),
        out_shape=jax.ShapeDtypeStruct((t, ATTN_WIDTH), BF16),
        compiler_params=_cparams(("parallel", "parallel")),
        name="swa",
    )(sinks, p, p, p, p, _alibi_bias_tables())


def _merge_kernel(y_ref, a_ref, gs_ref, ga_ref, wso_ref, wao_ref, o_ref):
    y_ssm = _dot(y_ref[...], wso_ref[...])
    y_attn = _dot(a_ref[...], wao_ref[...])
    o_ref[...] = (gs_ref[...] * y_ssm + ga_ref[...] * y_attn).astype(o_ref.dtype)


def _merge(y, a, p, wso, wao, tm, tn):
    t = y.shape[0]
    gs_blk0 = P_SGATE // tn
    ga_blk0 = (P_SGATE + D_MODEL) // tn
    return pl.pallas_call(
        _merge_kernel,
        grid=(D_MODEL // tn, t // tm),
        in_specs=[
            pl.BlockSpec((tm, D_INNER), lambda j, i: (i, 0)),
            pl.BlockSpec((tm, ATTN_WIDTH), lambda j, i: (i, 0)),
            pl.BlockSpec((tm, tn), lambda j, i: (i, gs_blk0 + j)),
            pl.BlockSpec((tm, tn), lambda j, i: (i, ga_blk0 + j)),
            pl.BlockSpec((D_INNER, tn), lambda j, i: (0, j)),
            pl.BlockSpec((ATTN_WIDTH, tn), lambda j, i: (0, j)),
        ],
        out_specs=pl.BlockSpec((tm, tn), lambda j, i: (i, j)),
        out_shape=jax.ShapeDtypeStruct((t, D_MODEL), BF16),
        compiler_params=_cparams(("parallel", "parallel")),
        name="merge",
    )(y, a, p, p, wso, wao)


def _oproj_kernel(m_ref, w_ref, x_ref, o_ref):
    o_ref[...] = x_ref[...] + _dot(m_ref[...], w_ref[...])


def _oproj(merged, wo, x2, tm):
    t = x2.shape[0]
    return pl.pallas_call(
        _oproj_kernel,
        grid=(t // tm,),
        in_specs=[
            pl.BlockSpec((tm, D_MODEL), lambda i: (i, 0)),
            pl.BlockSpec((D_MODEL, D_MODEL), lambda i: (0, 0)),
            pl.BlockSpec((tm, D_MODEL), lambda i: (i, 0)),
        ],
        out_specs=pl.BlockSpec((tm, D_MODEL), lambda i: (i, 0)),
        out_shape=jax.ShapeDtypeStruct((t, D_MODEL), F32),
        compiler_params=_cparams(("parallel",)),
        name="oproj",
    )(merged, wo, x2)


def _ffn_kernel(tiles_per_seq, h_ref, halo_ref, nw_ref, wg_ref, wv_ref, cw_ref, cb_ref, wout_ref,
                fw_ref, o_ref, u_scr, acc_scr):
    i = pl.program_id(0)
    j = pl.program_id(1)
    tm = h_ref.shape[0]

    @pl.when(j == 0)
    def _():
        nw = nw_ref[...]
        first = (i % tiles_per_seq) == 0
        u_scr[0:HALO, :] = jnp.where(first, 0.0, _rms_scale(halo_ref[...], nw)).astype(BF16)
        u_scr[HALO:, :] = _rms_scale(h_ref[...], nw).astype(BF16)
        acc_scr[...] = jnp.zeros_like(acc_scr)

    gate = _dot(u_scr[...], wg_ref[...])
    val = _dot(u_scr[HALO:, :], wv_ref[...])
    conv = _causal_conv(gate, cw_ref[...], cb_ref[...], FFN_CONV, tm)
    act = 0.5 * conv * (1.0 + lax.erf(conv * (2.0 ** -0.5)))
    acc_scr[...] += _dot((act * val).astype(BF16), wout_ref[...])

    @pl.when(j == pl.num_programs(1) - 1)
    def _():
        o_ref[...] = _rms_scale(h_ref[...] + acc_scr[...], fw_ref[...])


def _ffn(h1, nw, w_in, cw, cb, wout, fw, seq, tm, tn):
    t = h1.shape[0]
    halo_blocks = tm // HALO
    val_blk0 = D_FF // tn
    return pl.pallas_call(
        functools.partial(_ffn_kernel, seq // tm),
        grid=(t // tm, D_FF // tn),
        in_specs=[
            pl.BlockSpec((tm, D_MODEL), lambda i, j: (i, 0)),
            pl.BlockSpec((HALO, D_MODEL), lambda i, j: (jnp.maximum(i * halo_blocks - 1, 0), 0)),
            pl.BlockSpec((1, D_MODEL), lambda i, j: (0, 0)),
            pl.BlockSpec((D_MODEL, tn), lambda i, j: (0, j)),
            pl.BlockSpec((D_MODEL, tn), lambda i, j: (0, val_blk0 + j)),
            pl.BlockSpec((None, FFN_CONV, tn), lambda i, j: (0, 0, j)),
            pl.BlockSpec((1, tn), lambda i, j: (0, j)),
            pl.BlockSpec((tn, D_MODEL), lambda i, j: (j, 0)),
            pl.BlockSpec((1, D_MODEL), lambda i, j: (0, 0)),
        ],
        out_specs=pl.BlockSpec((tm, D_MODEL), lambda i, j: (i, 0)),
        out_shape=jax.ShapeDtypeStruct((t, D_MODEL), F32),
        scratch_shapes=[
            pltpu.VMEM((HALO + tm, D_MODEL), BF16),
            pltpu.VMEM((tm, D_MODEL), F32),
        ],
        compiler_params=_cparams(("parallel", "arbitrary")),
        name="conv_ffn",
    )(h1, h1, nw, w_in, w_in, cw, cb, wout, fw)


def _packed_head_expand():
    e = np.zeros((2 * LANE, D_INNER), np.float32)
    for part in range(3):
        for h in range(SSM_HEADS):
            e[part * SSM_HEADS + h, h * SSM_HEAD_DIM:(h + 1) * SSM_HEAD_DIM] = 1.0
    return jnp.asarray(e, BF16)


def _pad_lanes(v, width):
    v = v.astype(F32).reshape(1, -1)
    return jnp.pad(v, ((0, 0), (0, width - v.shape[1])))


def kernel(x, attn_norm_w, w_in, ssm_conv_w, ssm_conv_b, dt_bias, a_log, d_skip, ssm_norm_w, attn_sinks, w_ssm_out, w_attn_out, w_o, ffn_norm_w, w_ffn_in, ffn_conv_w, ffn_conv_b, w_ffn_out, final_norm_w):
    batch, seq, d = x.shape
    t = batch * seq
    assert w_in.shape[0] == 1, "single-layer problem: the final norm is fused into the FFN kernel"
    tm = min(1024, seq)
    tm_half = min(512, seq)
    assert seq % tm == 0 and seq % (SSD_CHUNKS_PER_STEP * CHUNK) == 0
    x2 = x.reshape(t, d)

    u = _norm(x2, attn_norm_w[0].reshape(1, d), tm)
    w_in_t = jnp.swapaxes(w_in, 1, 2)[0]
    p = _in_proj(u, w_in_t, ssm_conv_w, ssm_conv_b, seq, tm)

    dtb = _pad_lanes(dt_bias[0], LANE)
    a_neg = _pad_lanes(-jnp.exp(a_log[0].astype(F32)), LANE)
    dsk = jnp.repeat(d_skip[0].astype(F32), SSM_HEAD_DIM).reshape(1, D_INNER)
    snw = ssm_norm_w[0].astype(F32).reshape(1, D_INNER)
    y = _ssd(p, dtb, a_neg, _packed_head_expand(), dsk, snw, batch, seq)

    attn = _attention(p, attn_sinks[0].astype(F32), batch, seq)

    merged = _merge(y, attn, p, w_ssm_out[0].astype(BF16), w_attn_out[0].astype(BF16), tm_half, 1024)
    h1 = _oproj(merged, w_o[0].astype(BF16), x2, tm_half)

    out = _ffn(h1, ffn_norm_w[0].reshape(1, d), w_ffn_in[0].astype(BF16), ffn_conv_w, ffn_conv_b,
               w_ffn_out[0].astype(BF16), final_norm_w.reshape(1, d), seq, tm_half, 512)
    return out.reshape(batch, seq, d)
```

```python
import functools
import math

import numpy as np
import jax
import jax.numpy as jnp
from jax import lax
from jax.experimental import pallas as pl
from jax.experimental.pallas import tpu as pltpu

F32 = jnp.float32
BF16 = jnp.bfloat16

D_MODEL = 2048
D_INNER = 4096
SSM_HEADS = 64
SSM_HEAD_DIM = 64
SSM_GROUPS = 8
SSM_STATE = 128
SSM_CONV = 4
CHUNK = 128
SSD_CHUNKS_PER_STEP = 2
SWA_BLOCKS_PER_STEP = 4
GROUP_WIDTH = D_INNER // SSM_GROUPS
HEADS_PER_GROUP = SSM_HEADS // SSM_GROUPS
SSM_GN = SSM_GROUPS * SSM_STATE
ATTN_HEADS = 32
ATTN_KV_HEADS = 4
ATTN_GROUP = ATTN_HEADS // ATTN_KV_HEADS
ATTN_HEAD_DIM = 64
ATTN_WIDTH = ATTN_HEADS * ATTN_HEAD_DIM
KV_WIDTH = ATTN_KV_HEADS * ATTN_HEAD_DIM
WINDOW = 128
D_FF = 5632
FFN_CONV = 3
NORM_EPS = 1e-5
LOG2E = math.log2(math.e)

P_Z = 0
P_XS = D_INNER
P_B = 2 * D_INNER
P_C = P_B + SSM_GN
P_DT = P_C + SSM_GN
P_Q = P_DT + SSM_HEADS
P_K = P_Q + ATTN_WIDTH
P_V = P_K + KV_WIDTH
P_GATE = P_V + KV_WIDTH
P_TN = 1024
P_SUB = 256
P_CONV_BLK0 = P_XS // P_TN
P_CONV_BLK1 = P_DT // P_TN
P_RAW_BLK1 = pl.cdiv(P_GATE, P_TN)
P_SGATE = P_RAW_BLK1 * P_TN
P_WIDTH = P_SGATE + 2 * D_MODEL

LANE = 128
HALO = 16
SHIFT = P_Q % LANE
VMEM_LIMIT = 56 * 1024 * 1024


def _cparams(semantics):
    return pltpu.CompilerParams(dimension_semantics=semantics, vmem_limit_bytes=VMEM_LIMIT)


def _dot(a, b):
    return jnp.dot(a, b, preferred_element_type=F32)


def _dot_nt(a, b):
    return lax.dot_general(a, b, (((1,), (1,)), ((), ())), preferred_element_type=F32)


def _split3(v):
    hi = v.astype(BF16).astype(F32)
    r1 = v - hi
    mid = r1.astype(BF16).astype(F32)
    lo = (r1 - mid).astype(BF16).astype(F32)
    return hi, mid, lo


def _rms_scale(x, w):
    ms = jnp.mean(x * x, axis=-1, keepdims=True)
    return x * lax.rsqrt(ms + NORM_EPS) * w


def _sigmoid(v):
    return 0.5 * jnp.tanh(0.5 * v) + 0.5


def _silu_of_half(h):
    return h + h * jnp.tanh(h)


def _causal_conv(ext, w, b, taps, rows):
    acc = b + w[taps - 1:taps] * ext[HALO:HALO + rows]
    for k in range(taps - 1):
        lo = HALO - (taps - 1) + k
        acc = acc + w[k:k + 1] * ext[lo:lo + rows]
    return acc


def _norm_kernel(x_ref, w_ref, o_ref):
    o_ref[...] = _rms_scale(x_ref[...], w_ref[...]).astype(o_ref.dtype)


def _norm(x2, w, tm):
    t, d = x2.shape
    return pl.pallas_call(
        _norm_kernel,
        grid=(t // tm,),
        in_specs=[pl.BlockSpec((tm, d), lambda i: (i, 0)), pl.BlockSpec((1, d), lambda i: (0, 0))],
        out_specs=pl.BlockSpec((tm, d), lambda i: (i, 0)),
        out_shape=jax.ShapeDtypeStruct((t, d), BF16),
        compiler_params=_cparams(("parallel",)),
        name="norm",
    )(x2, w)


def _in_proj_kernel(tiles_per_seq, u_ref, halo_ref, w_ref, cw_ref, cb_ref, o_ref, w_scr, lhs_scr):
    j = pl.program_id(0)
    i = pl.program_id(1)
    tm = u_ref.shape[0]
    sub = [slice(s * P_SUB, (s + 1) * P_SUB) for s in range(P_TN // P_SUB)]

    @pl.when(i == 0)
    def _():
        scale = jnp.where(j < P_CONV_BLK1, 0.5, 1.0)
        w_scr[...] = (w_ref[...] * scale).T.astype(BF16)

    def direct(epilogue, tiles=sub):
        for cols in tiles:
            o_ref[:, cols] = epilogue(_dot(u_ref[...], w_scr[:, cols]))

    @pl.when(j < P_CONV_BLK0)
    def _():
        direct(_silu_of_half)

    @pl.when((j >= P_CONV_BLK1) & (j < P_RAW_BLK1 - 1))
    def _():
        direct(lambda acc: acc)

    @pl.when(j == P_RAW_BLK1 - 1)
    def _():
        used = pl.cdiv(P_GATE - (P_RAW_BLK1 - 1) * P_TN, P_SUB)
        direct(lambda acc: acc, sub[:used])
        for cols in sub[used:]:
            o_ref[:, cols] = jnp.zeros((tm, P_SUB), F32)

    @pl.when(j >= P_RAW_BLK1)
    def _():
        direct(_sigmoid)

    @pl.when((j >= P_CONV_BLK0) & (j < P_CONV_BLK1))
    def _():
        first = (i % tiles_per_seq) == 0
        halo = halo_ref[...]
        lhs_scr[0:HALO, :] = jnp.where(first, jnp.zeros_like(halo), halo)
        lhs_scr[HALO:, :] = u_ref[...]
        for cols in sub:
            half_ext = _dot(lhs_scr[...], w_scr[:, cols])
            half_conv = _causal_conv(half_ext, cw_ref[:, cols], 0.5 * cb_ref[:, cols], SSM_CONV, tm)
            o_ref[:, cols] = _silu_of_half(half_conv)


def _in_proj(u, w_in_t, cw, cb, seq, tm):
    t, d = u.shape
    halo_blocks = tm // HALO
    conv_blk = lambda j: jnp.clip(j - P_CONV_BLK0, 0, P_CONV_BLK1 - P_CONV_BLK0 - 1)
    w_row = lambda j: pl.multiple_of(
        jnp.where(j < P_RAW_BLK1, j * P_TN, P_GATE + (j - P_RAW_BLK1) * P_TN), math.gcd(P_TN, P_GATE))
    return pl.pallas_call(
        functools.partial(_in_proj_kernel, seq // tm),
        grid=(P_WIDTH // P_TN, t // tm),
        in_specs=[
            pl.BlockSpec((tm, d), lambda j, i: (i, 0)),
            pl.BlockSpec((HALO, d), lambda j, i: (jnp.maximum(i * halo_blocks - 1, 0), 0)),
            pl.BlockSpec((pl.Element(P_TN), pl.Element(d)), lambda j, i: (w_row(j), 0)),
            pl.BlockSpec((None, SSM_CONV, P_TN), lambda j, i: (0, 0, conv_blk(j))),
            pl.BlockSpec((1, P_TN), lambda j, i: (0, conv_blk(j))),
        ],
        out_specs=pl.BlockSpec((tm, P_TN), lambda j, i: (i, j)),
        out_shape=jax.ShapeDtypeStruct((t, P_WIDTH), F32),
        scratch_shapes=[pltpu.VMEM((d, P_TN), BF16), pltpu.VMEM((HALO + tm, d), BF16)],
        compiler_params=_cparams(("parallel", "arbitrary")),
        name="in_proj",
    )(u, u, w_in_t, cw, cb)


def _ssd_kernel(sz_ref, xs_ref, bc_ref, dt_ref, dtb_ref, a_ref, epk_ref, dsk_ref, nw_ref,
                y_ref, state_scr):
    @pl.when(pl.program_id(1) == 0)
    def _():
        state_scr[...] = jnp.zeros_like(state_scr)

    for sub in range(SSD_CHUNKS_PER_STEP):
        rows = slice(sub * CHUNK, (sub + 1) * CHUNK)
        _ssd_chunk(sz_ref.at[rows], xs_ref.at[rows], bc_ref.at[rows], dt_ref.at[rows], dtb_ref, a_ref, epk_ref,
                   dsk_ref, nw_ref, y_ref.at[rows], state_scr)


def _ssd_chunk(sz_ref, xs_ref, bc_ref, dt_ref, dtb_ref, a_ref, epk_ref, dsk_ref, nw_ref,
               y_ref, state_scr):
    v = dt_ref[...] + dtb_ref[...]
    dt = jnp.maximum(v, 0.0) + jnp.log1p(jnp.exp(-jnp.abs(v)))
    ad = dt * a_ref[...]

    row = lax.broadcasted_iota(jnp.int32, (CHUNK, CHUNK), 0)
    col = lax.broadcasted_iota(jnp.int32, (CHUNK, CHUNK), 1)
    causal = row >= col
    tril = causal.astype(BF16)
    a_cum = sum(_dot(tril, p.astype(BF16)) for p in _split3(ad)) * LOG2E
    a_cum_t = a_cum.T

    low_half = col < SSM_HEADS

    def packed(q):
        hi, mid, lo = _split3(q)
        first = jnp.where(low_half, hi, pltpu.roll(mid, SSM_HEADS, axis=1))
        return jnp.concatenate([first, lo], axis=1).astype(BF16)

    stack = jnp.concatenate([packed(dt), packed(a_cum)], axis=0)

    lane_low = lax.broadcasted_iota(jnp.int32, (CHUNK, LANE), 1) < SSM_HEAD_DIM
    neg_inf = jnp.float32(-jnp.inf)
    for g in range(SSM_GROUPS):
        c0 = g * GROUP_WIDTH
        expanded = _dot(stack, epk_ref[:, c0:c0 + GROUP_WIDTH])
        dtx = expanded[0:CHUNK]
        acx = expanded[CHUNK:2 * CHUNK]
        eax = jnp.exp2(acx)
        dsx = jnp.exp2(acx[CHUNK - 1:CHUNK, :] - acx)
        xs = xs_ref[:, c0:c0 + GROUP_WIDTH]
        bm = bc_ref[:, g * SSM_STATE:(g + 1) * SSM_STATE]
        cm16 = bc_ref[:, SSM_GN + g * SSM_STATE:SSM_GN + (g + 1) * SSM_STATE].astype(BF16)
        xdt = xs * dtx
        cb = _dot_nt(cm16, bm.astype(BF16))

        y_parts = []
        for pr in range(HEADS_PER_GROUP // 2):
            ms = []
            for h in (g * HEADS_PER_GROUP + 2 * pr, g * HEADS_PER_GROUP + 2 * pr + 1):
                seg = a_cum[:, h:h + 1] - a_cum_t[h:h + 1, :]
                decay = jnp.exp2(jnp.where(causal, seg, neg_inf))
                ms.append((cb * decay).astype(BF16))
            m_pair = jnp.concatenate(ms, axis=1)
            xp = xdt[:, pr * LANE:(pr + 1) * LANE]
            x_pair = jnp.concatenate([jnp.where(lane_low, xp, 0.0), jnp.where(lane_low, 0.0, xp)],
                                     axis=0).astype(BF16)
            y_parts.append(_dot(m_pair, x_pair))
        y = jnp.concatenate(y_parts, axis=1)

        state = state_scr[g]
        y = y + _dot(cm16, state.astype(BF16)) * eax
        new_state = _dot(bm.T.astype(BF16), (xdt * dsx).astype(BF16))
        state_scr[g] = state * eax[CHUNK - 1:CHUNK, :] + new_state

        y = (y + dsk_ref[:, c0:c0 + GROUP_WIDTH] * xs) * sz_ref[:, c0:c0 + GROUP_WIDTH]
        y_ref[:, c0:c0 + GROUP_WIDTH] = _rms_scale(y, nw_ref[:, c0:c0 + GROUP_WIDTH]).astype(y_ref.dtype)


def _ssd(p, dtb, a_neg, epk, dsk, nw, batch, seq):
    t = p.shape[0]
    rows_per_step = SSD_CHUNKS_PER_STEP * CHUNK
    nc = seq // rows_per_step

    def rows(b, c):
        return b * nc + c

    const = lambda shape: pl.BlockSpec(shape, lambda b, c: (0, 0))
    return pl.pallas_call(
        _ssd_kernel,
        grid=(batch, nc),
        in_specs=[
            pl.BlockSpec((rows_per_step, D_INNER), lambda b, c: (rows(b, c), P_Z // D_INNER)),
            pl.BlockSpec((rows_per_step, D_INNER), lambda b, c: (rows(b, c), P_XS // D_INNER)),
            pl.BlockSpec((rows_per_step, 2 * SSM_GN), lambda b, c: (rows(b, c), P_B // (2 * SSM_GN))),
            pl.BlockSpec((rows_per_step, LANE), lambda b, c: (rows(b, c), P_DT // LANE)),
            const((1, LANE)), const((1, LANE)),
            const((2 * LANE, D_INNER)),
            const((1, D_INNER)), const((1, D_INNER)),
        ],
        out_specs=pl.BlockSpec((rows_per_step, D_INNER), lambda b, c: (rows(b, c), 0)),
        out_shape=jax.ShapeDtypeStruct((t, D_INNER), BF16),
        scratch_shapes=[pltpu.VMEM((SSM_GROUPS, SSM_STATE, GROUP_WIDTH), F32)],
        compiler_params=_cparams(("parallel", "arbitrary")),
        name="ssd",
    )(p, p, p, p, dtb, a_neg, epk, dsk, nw)


Q_WIN0 = P_Q - SHIFT
KV_WIN0 = (P_K // P_TN) * P_TN
K_LOC = P_K - KV_WIN0
V_LOC = P_V - KV_WIN0
V_SLAB0 = (V_LOC // LANE) * LANE
V_SLAB1 = V_SLAB0 + KV_WIDTH + LANE


def _attn_kernel(sink_ref, q_ref, qx_ref, kvp_ref, kvc_ref, bias_ref, o_ref):
    first_table = jnp.minimum(pl.program_id(1), 1)
    for s in range(SWA_BLOCKS_PER_STEP):
        rows = slice(s * WINDOW, (s + 1) * WINDOW)
        prev_ref = kvp_ref if s == 0 else kvc_ref.at[slice((s - 1) * WINDOW, s * WINDOW)]
        table = first_table if s == 0 else 1
        _attn_block(sink_ref, q_ref.at[rows], qx_ref.at[rows], prev_ref, kvc_ref.at[rows],
                    bias_ref.at[table], o_ref.at[rows])


def _attn_block(sink_ref, q_ref, qx_ref, kvp_ref, kvc_ref, bias_ref, o_ref):
    c = (ATTN_HEAD_DIM ** -0.5) * LOG2E
    key_j = lax.broadcasted_iota(jnp.int32, (WINDOW, WINDOW), 0)
    query_l = lax.broadcasted_iota(jnp.int32, (WINDOW, WINDOW), 1)
    from_prev = key_j > query_l
    kv = jnp.concatenate([kvp_ref[...], kvc_ref[...]], axis=0)
    v_t = kv[:, V_SLAB0:V_SLAB1].T.astype(BF16)
    q_t = jnp.concatenate([q_ref[...].T, qx_ref[...].T], axis=0).astype(BF16)
    for kh in range(ATTN_KV_HEADS):
        d0 = kh * ATTN_HEAD_DIM
        k_h = kv[:, K_LOC + d0:K_LOC + d0 + ATTN_HEAD_DIM].astype(BF16)
        heads = [kh * ATTN_GROUP + r for r in range(ATTN_GROUP)]
        q_g = jnp.concatenate(
            [q_t[SHIFT + h * ATTN_HEAD_DIM:SHIFT + (h + 1) * ATTN_HEAD_DIM, :] for h in heads],
            axis=1)
        s_t = _dot(k_h, q_g)
        prev_ps, cur_ps, dens = [], [], []
        for r, h in enumerate(heads):
            s_h = s_t[:, r * WINDOW:(r + 1) * WINDOW]
            l2 = jnp.where(from_prev, s_h[:WINDOW], s_h[WINDOW:]) * c + bias_ref[h]
            sink2 = sink_ref[h] * LOG2E
            m = jnp.maximum(jnp.max(l2, axis=0, keepdims=True), sink2)
            p = jnp.exp2(l2 - m)
            dens.append(jnp.sum(p, axis=0, keepdims=True) + jnp.exp2(sink2 - m))
            prev_ps.append(jnp.where(from_prev, p, 0.0).astype(BF16))
            cur_ps.append(jnp.where(from_prev, 0.0, p).astype(BF16))
        p_t = jnp.concatenate([jnp.concatenate(prev_ps, axis=1), jnp.concatenate(cur_ps, axis=1)],
                              axis=0)
        v0 = V_LOC - V_SLAB0 + d0
        o_t = _dot(v_t[v0:v0 + ATTN_HEAD_DIM, :], p_t)
        o_g = jnp.concatenate([o_t[:, r * WINDOW:(r + 1) * WINDOW] / dens[r]
                               for r in range(ATTN_GROUP)], axis=0)
        w0 = kh * ATTN_GROUP * ATTN_HEAD_DIM
        o_ref[:, w0:w0 + ATTN_GROUP * ATTN_HEAD_DIM] = o_g.T.astype(o_ref.dtype)


def _alibi_bias_tables():
    j = jnp.arange(WINDOW)[:, None]
    l = jnp.arange(WINDOW)[None, :]
    from_prev = j > l
    dist = jnp.where(from_prev, l - j + WINDOW, l - j).astype(F32)
    slopes = jnp.exp2(-8.0 / ATTN_HEADS * jnp.arange(1, ATTN_HEADS + 1, dtype=F32))
    bias = -slopes[:, None, None] * dist * LOG2E
    first = jnp.where(from_prev[None], -jnp.inf, bias)
    return jnp.stack([first, bias])


def _attention(p, sinks, batch, seq):
    t = p.shape[0]
    rows_per_step = SWA_BLOCKS_PER_STEP * WINDOW
    ns = seq // rows_per_step

    def cur(b, n):
        return b * ns + n

    def prev(b, n):
        return (b * ns + n) * SWA_BLOCKS_PER_STEP - jnp.minimum(n, 1)

    return pl.pallas_call(
        _attn_kernel,
        grid=(batch, ns),
        in_specs=[
            pl.BlockSpec(memory_space=pltpu.SMEM),
            pl.BlockSpec((rows_per_step, ATTN_WIDTH), lambda b, n: (cur(b, n), Q_WIN0 // ATTN_WIDTH)),
            pl.BlockSpec((rows_per_step, LANE), lambda b, n: (cur(b, n), (Q_WIN0 + ATTN_WIDTH) // LANE)),
            pl.BlockSpec((WINDOW, P_TN), lambda b, n: (prev(b, n), KV_WIN0 // P_TN)),
            pl.BlockSpec((rows_per_step, P_TN), lambda b, n: (cur(b, n), KV_WIN0 // P_TN)),
            pl.BlockSpec((2, ATTN_HEADS, WINDOW, WINDOW), lambda b, n: (0, 0, 0, 0)),
        ],
        out_specs=pl.BlockSpec((rows_per_step, ATTN_WIDTH), lambda b, n: (cur(b, n), 0)),
        out_shape=jax.ShapeDtypeStruct((t, ATTN_WIDTH), BF16),
        compiler_params=_cparams(("parallel", "parallel")),
        name="swa",
    )(sinks, p, p, p, p, _alibi_bias_tables())


def _merge_kernel(y_ref, a_ref, gs_ref, ga_ref, wso_ref, wao_ref, o_ref):
    y_ssm = _dot(y_ref[...], wso_ref[...])
    y_attn = _dot(a_ref[...], wao_ref[...])
    o_ref[...] = (gs_ref[...] * y_ssm + ga_ref[...] * y_attn).astype(o_ref.dtype)


def _merge(y, a, p, wso, wao, tm, tn):
    t = y.shape[0]
    gs_blk0 = P_SGATE // tn
    ga_blk0 = (P_SGATE + D_MODEL) // tn
    return pl.pallas_call(
        _merge_kernel,
        grid=(D_MODEL // tn, t // tm),
        in_specs=[
            pl.BlockSpec((tm, D_INNER), lambda j, i: (i, 0)),
            pl.BlockSpec((tm, ATTN_WIDTH), lambda j, i: (i, 0)),
            pl.BlockSpec((tm, tn), lambda j, i: (i, gs_blk0 + j)),
            pl.BlockSpec((tm, tn), lambda j, i: (i, ga_blk0 + j)),
            pl.BlockSpec((D_INNER, tn), lambda j, i: (0, j)),
            pl.BlockSpec((ATTN_WIDTH, tn), lambda j, i: (0, j)),
        ],
        out_specs=pl.BlockSpec((tm, tn), lambda j, i: (i, j)),
        out_shape=jax.ShapeDtypeStruct((t, D_MODEL), BF16),
        compiler_params=_cparams(("parallel", "parallel")),
        name="merge",
    )(y, a, p, p, wso, wao)


def _oproj_kernel(m_ref, w_ref, x_ref, o_ref):
    o_ref[...] = x_ref[...] + _dot(m_ref[...], w_ref[...])


def _oproj(merged, wo, x2, tm):
    t = x2.shape[0]
    return pl.pallas_call(
        _oproj_kernel,
        grid=(t // tm,),
        in_specs=[
            pl.BlockSpec((tm, D_MODEL), lambda i: (i, 0)),
            pl.BlockSpec((D_MODEL, D_MODEL), lambda i: (0, 0)),
            pl.BlockSpec((tm, D_MODEL), lambda i: (i, 0)),
        ],
        out_specs=pl.BlockSpec((tm, D_MODEL), lambda i: (i, 0)),
        out_shape=jax.ShapeDtypeStruct((t, D_MODEL), F32),
        compiler_params=_cparams(("parallel",)),
        name="oproj",
    )(merged, wo, x2)


def _ffn_kernel(tiles_per_seq, h_ref, halo_ref, nw_ref, wg_ref, wv_ref, cw_ref, cb_ref, wout_ref,
                fw_ref, o_ref, u_scr, acc_scr):
    i = pl.program_id(0)
    j = pl.program_id(1)
    tm = h_ref.shape[0]

    @pl.when(j == 0)
    def _():
        nw = nw_ref[...]
        first = (i % tiles_per_seq) == 0
        u_scr[0:HALO, :] = jnp.where(first, 0.0, _rms_scale(halo_ref[...], nw)).astype(BF16)
        u_scr[HALO:, :] = _rms_scale(h_ref[...], nw).astype(BF16)
        acc_scr[...] = jnp.zeros_like(acc_scr)

    gate = _dot(u_scr[...], wg_ref[...])
    val = _dot(u_scr[HALO:, :], wv_ref[...])
    conv = _causal_conv(gate, cw_ref[...], cb_ref[...], FFN_CONV, tm)
    act = 0.5 * conv * (1.0 + lax.erf(conv * (2.0 ** -0.5)))
    acc_scr[...] += _dot((act * val).astype(BF16), wout_ref[...])

    @pl.when(j == pl.num_programs(1) - 1)
    def _():
        o_ref[...] = _rms_scale(h_ref[...] + acc_scr[...], fw_ref[...])


def _ffn(h1, nw, w_in, cw, cb, wout, fw, seq, tm, tn):
    t = h1.shape[0]
    halo_blocks = tm // HALO
    val_blk0 = D_FF // tn
    return pl.pallas_call(
        functools.partial(_ffn_kernel, seq // tm),
        grid=(t // tm, D_FF // tn),
        in_specs=[
            pl.BlockSpec((tm, D_MODEL), lambda i, j: (i, 0)),
            pl.BlockSpec((HALO, D_MODEL), lambda i, j: (jnp.maximum(i * halo_blocks - 1, 0), 0)),
            pl.BlockSpec((1, D_MODEL), lambda i, j: (0, 0)),
            pl.BlockSpec((D_MODEL, tn), lambda i, j: (0, j)),
            pl.BlockSpec((D_MODEL, tn), lambda i, j: (0, val_blk0 + j)),
            pl.BlockSpec((None, FFN_CONV, tn), lambda i, j: (0, 0, j)),
            pl.BlockSpec((1, tn), lambda i, j: (0, j)),
            pl.BlockSpec((tn, D_MODEL), lambda i, j: (j, 0)),
            pl.BlockSpec((1, D_MODEL), lambda i, j: (0, 0)),
        ],
        out_specs=pl.BlockSpec((tm, D_MODEL), lambda i, j: (i, 0)),
        out_shape=jax.ShapeDtypeStruct((t, D_MODEL), F32),
        scratch_shapes=[
            pltpu.VMEM((HALO + tm, D_MODEL), BF16),
            pltpu.VMEM((tm, D_MODEL), F32),
        ],
        compiler_params=_cparams(("parallel", "arbitrary")),
        name="conv_ffn",
    )(h1, h1, nw, w_in, w_in, cw, cb, wout, fw)


def _packed_head_expand():
    e = np.zeros((2 * LANE, D_INNER), np.float32)
    for part in range(3):
        for h in range(SSM_HEADS):
            e[part * SSM_HEADS + h, h * SSM_HEAD_DIM:(h + 1) * SSM_HEAD_DIM] = 1.0
    return jnp.asarray(e, BF16)


def _pad_lanes(v, width):
    v = v.astype(F32).reshape(1, -1)
    return jnp.pad(v, ((0, 0), (0, width - v.shape[1])))


def kernel(x, attn_norm_w, w_in, ssm_conv_w, ssm_conv_b, dt_bias, a_log, d_skip, ssm_norm_w, attn_sinks, w_ssm_out, w_attn_out, w_o, ffn_norm_w, w_ffn_in, ffn_conv_w, ffn_conv_b, w_ffn_out, final_norm_w):
    batch, seq, d = x.shape
    t = batch * seq
    assert w_in.shape[0] == 1, "single-layer problem: the final norm is fused into the FFN kernel"
    tm = min(1024, seq)
    tm_half = min(512, seq)
    assert seq % tm == 0 and seq % (SSD_CHUNKS_PER_STEP * CHUNK) == 0
    x2 = x.reshape(t, d)

    u = _norm(x2, attn_norm_w[0].reshape(1, d), tm)
    w_in_t = jnp.swapaxes(w_in, 1, 2)[0]
    p = _in_proj(u, w_in_t, ssm_conv_w, ssm_conv_b, seq, tm)

    dtb = _pad_lanes(dt_bias[0], LANE)
    a_neg = _pad_lanes(-jnp.exp(a_log[0].astype(F32)), LANE)
    dsk = jnp.repeat(d_skip[0].astype(F32), SSM_HEAD_DIM).reshape(1, D_INNER)
    snw = ssm_norm_w[0].astype(F32).reshape(1, D_INNER)
    y = _ssd(p, dtb, a_neg, _packed_head_expand(), dsk, snw, batch, seq)

    attn = _attention(p, attn_sinks[0].astype(F32), batch, seq)

    merged = _merge(y, attn, p, w_ssm_out[0].astype(BF16), w_attn_out[0].astype(BF16), tm_half, 1024)
    h1 = _oproj(merged, w_o[0].astype(BF16), x2, tm_half)

    out = _ffn(h1, ffn_norm_w[0].reshape(1, d), w_ffn_in[0].astype(BF16), ffn_conv_w, ffn_conv_b,
               w_ffn_out[0].astype(BF16), final_norm_w.reshape(1, d), seq, tm_half, 512)
    return out.reshape(batch, seq, d)
```
